```python
import functools
import jax, jax.numpy as jnp
from jax import lax
import numpy as np

D_MODEL = 4096
BATCH = 4
SEQ = 2048
DEPTH = 4
DEC_BATCH = 8
DEC_SEQ = 1
PAST_LEN = 8192
PAGE_SIZE = 128

HD_A = 128
H_A = D_MODEL // HD_A
W_A = H_A * HD_A
QKV_A = 3 * W_A
CONV_W = 4
CHUNK = 64
HD_B = 128
H_B = D_MODEL // HD_B
KV_B = H_B // 2
G_B = H_B // KV_B
W_B = H_B * HD_B
KVW_B = KV_B * HD_B
Q_BLOCK = 128
ATT_SCALE = HD_B ** -0.5
D_FF = ((8 * D_MODEL // 3 + 255) // 256) * 256
NORM_EPS = 1e-6
L2_EPS = 1e-6
FORGET_BIAS_MEAN = 3.0
IN_WIDTHS = (QKV_A, W_B, KVW_B, KVW_B, H_A, H_A, H_B, D_MODEL, D_MODEL)
N_IN = QKV_A + W_B + 2 * KVW_B + 2 * H_A + H_B + 2 * D_MODEL

kernel_name = "hybrid_gdn_fox_adaln_decoder_step"


def rms_norm(x, g):
    xf = x.astype(jnp.float32)
    y = xf * lax.rsqrt(jnp.mean(xf * xf, axis=-1, keepdims=True) + NORM_EPS)
    return (y * g.astype(jnp.float32)).astype(x.dtype)


def l2_normalize(t):
    tf = t.astype(jnp.float32)
    return tf * lax.rsqrt(jnp.sum(tf * tf, axis=-1, keepdims=True) + L2_EPS)


def split_cols(z, widths):
    bounds, acc = [], 0
    for w in widths[:-1]:
        acc += w
        bounds.append(acc)
    return jnp.split(z, bounds, axis=-1)


def adaln(c, w_ada, b_ada):
    m = (jax.nn.silu(c) @ w_ada + b_ada)[:, None, :]
    return tuple(jnp.split(m, 6, axis=-1))


def causal_conv(u, buf, w):
    L = u.shape[1]
    full = jnp.concatenate([buf, u], axis=1)
    y = full[:, 0:L] * w[0]
    for i in range(1, CONV_W):
        y = y + full[:, i:i + L] * w[i]
    return y, full[:, -(CONV_W - 1):]


def mixer_inputs(z, conv_buf, conv_w, a_log, dt_bias, b_f):
    B, L, _ = z.shape
    qkv_a, q_b, k_b, v_b, beta_raw, a_raw, f_raw, gate_a, gate_b = split_cols(z, IN_WIDTHS)
    qkv_a, conv_new = causal_conv(qkv_a, conv_buf, conv_w)
    t = jax.nn.silu(qkv_a).reshape(B, L, 3, H_A, HD_A)
    q_a = l2_normalize(t[:, :, 0]) * (HD_A ** -0.5)
    k_a = l2_normalize(t[:, :, 1])
    v_a = t[:, :, 2].astype(jnp.float32)
    beta = jax.nn.sigmoid(beta_raw.astype(jnp.float32))
    g = -jnp.exp(a_log.astype(jnp.float32)) * jax.nn.softplus(a_raw.astype(jnp.float32) + dt_bias.astype(jnp.float32))
    logf = jax.nn.log_sigmoid(f_raw.astype(jnp.float32) + b_f.astype(jnp.float32))
    q_b = q_b.reshape(B, L, KV_B, G_B, HD_B)
    k_b = k_b.reshape(B, L, KV_B, HD_B)
    v_b = v_b.reshape(B, L, KV_B, HD_B)
    return q_a, k_a, v_a, g, beta, q_b, k_b, v_b, logf, gate_a, gate_b, conv_new


def gated_delta_chunked(q, k, v, g, beta, s0):
    B, L, H, DK = q.shape
    DV = v.shape[-1]
    n = L // CHUNK

    def to_chunks(t):
        t = jnp.swapaxes(t, 1, 2)
        t = t.reshape(t.shape[:2] + (n, CHUNK) + t.shape[3:])
        return jnp.moveaxis(t, 2, 0)

    qc, kc, vc, gc, bc = (to_chunks(t) for t in (q, k, v, g, beta))
    G = jnp.cumsum(gc, axis=-1)
    idx = jnp.arange(CHUNK)
    incl = idx[:, None] >= idx[None, :]
    strict = idx[:, None] > idx[None, :]
    decay = jnp.exp(jnp.where(incl, G[..., :, None] - G[..., None, :], -jnp.inf))
    kb = kc * bc[..., None]
    a_mat = jnp.where(strict, jnp.einsum('nbhik,nbhjk->nbhij', kb, kc) * decay, 0.0)
    rhs = jnp.concatenate([vc * bc[..., None], kb * jnp.exp(G)[..., None]], axis=-1)
    sol = lax.linalg.triangular_solve(a_mat, rhs, left_side=True, lower=True, unit_diagonal=True)
    u, w = sol[..., :DV], sol[..., DV:]
    a_qk = jnp.einsum('nbhik,nbhjk->nbhij', qc, kc) * decay
    qg = qc * jnp.exp(G)[..., None]
    kd = kc * jnp.exp(G[..., -1:] - G)[..., None]
    gl = jnp.exp(G[..., -1])[..., None, None]

    def step(s, inp):
        u_i, w_i, aqk_i, qg_i, kd_i, gl_i = inp
        delta = u_i - jnp.einsum('bhck,bhkv->bhcv', w_i, s)
        o_i = jnp.einsum('bhck,bhkv->bhcv', qg_i, s) + jnp.einsum('bhij,bhjv->bhiv', aqk_i, delta)
        s = gl_i * s + jnp.einsum('bhck,bhcv->bhkv', kd_i, delta)
        return s, o_i

    s_fin, o = lax.scan(step, s0, (u, w, a_qk, qg, kd, gl))
    o = jnp.moveaxis(o, 0, 2).reshape(B, H, L, DV)
    return jnp.swapaxes(o, 1, 2), s_fin


def gated_delta_recurrent(q, k, v, g, beta, s0):
    def step(s, inp):
        q_t, k_t, v_t, g_t, b_t = inp
        s = jnp.exp(g_t)[..., None, None] * s
        delta = b_t[..., None] * (v_t - jnp.einsum('bhk,bhkv->bhv', k_t, s))
        s = s + k_t[..., :, None] * delta[..., None, :]
        return s, jnp.einsum('bhk,bhkv->bhv', q_t, s)

    xs = tuple(jnp.swapaxes(t, 0, 1) for t in (q, k, v, g, beta))
    s_fin, o = lax.scan(step, s0, xs)
    return jnp.swapaxes(o, 0, 1), s_fin


def fox_attend(q, cq, qpos, k, v, ck, kpos):
    s = jnp.einsum('bqkgd,bskd->bkgqs', q, k, preferred_element_type=jnp.float32) * ATT_SCALE
    bias = cq[..., :, None] - ck[..., None, :]
    s = jnp.where(kpos[None, :] <= qpos[:, None], s + bias, -jnp.inf)
    p = jax.nn.softmax(s, axis=-1)
    return jnp.einsum('bkgqs,bskd->bqkgd', p.astype(v.dtype), v)


def fox_prompt(q, k, v, logf):
    B, L = q.shape[:2]
    nb = L // Q_BLOCK
    C = jnp.cumsum(logf, axis=1).reshape(B, L, KV_B, G_B).transpose(0, 2, 3, 1)
    qb = jnp.moveaxis(q.reshape(B, nb, Q_BLOCK, KV_B, G_B, HD_B), 1, 0)
    cqb = jnp.moveaxis(C.reshape(B, KV_B, G_B, nb, Q_BLOCK), 3, 0)
    qpos = jnp.arange(L).reshape(nb, Q_BLOCK)
    kpos = jnp.arange(L)
    o = lax.map(lambda a: fox_attend(a[0], a[1], a[2], k, v, C, kpos), (qb, cqb, qpos))
    return jnp.moveaxis(o, 0, 1).reshape(B, L, KV_B, G_B, HD_B)


def fox_sample(q, k_new, v_new, logf_new, k_past, v_past, logf_past):
    B, T = q.shape[:2]
    P = k_past.shape[1]
    k = jnp.concatenate([k_past, k_new], axis=1)
    v = jnp.concatenate([v_past, v_new], axis=1)
    lf = jnp.concatenate([logf_past.astype(jnp.float32), logf_new], axis=1)
    C = jnp.cumsum(lf, axis=1).reshape(B, P + T, KV_B, G_B).transpose(0, 2, 3, 1)
    return fox_attend(q, C[..., P:], P + jnp.arange(T), k, v, C, jnp.arange(P + T))


def merge_branches(o_a, o_b, gate_a, gate_b, onorm_g):
    B, L = gate_a.shape[:2]
    o_a = rms_norm(o_a, onorm_g).reshape(B, L, W_A).astype(gate_a.dtype)
    o_b = o_b.reshape(B, L, W_B).astype(gate_a.dtype)
    return jax.nn.sigmoid(gate_a) * o_a + jax.nn.sigmoid(gate_b) * o_b


def prompt_mixers(z, conv_w, a_log, dt_bias, b_f, onorm_g):
    B = z.shape[0]
    conv0 = jnp.zeros((B, CONV_W - 1, QKV_A), z.dtype)
    q_a, k_a, v_a, g, beta, q_b, k_b, v_b, logf, gate_a, gate_b, conv_new = mixer_inputs(z, conv0, conv_w, a_log, dt_bias, b_f)
    s0 = jnp.zeros((B, H_A, HD_A, HD_A), jnp.float32)
    o_a, s_fin = gated_delta_chunked(q_a, k_a, v_a, g, beta, s0)
    o_b = fox_prompt(q_b, k_b, v_b, logf)
    return merge_branches(o_a, o_b, gate_a, gate_b, onorm_g), (k_b, v_b, logf, s_fin, conv_new)


def sample_mixers(z, conv_buf, s0, k_past, v_past, logf_past, conv_w, a_log, dt_bias, b_f, onorm_g):
    q_a, k_a, v_a, g, beta, q_b, k_b, v_b, logf, gate_a, gate_b, conv_new = mixer_inputs(z, conv_buf, conv_w, a_log, dt_bias, b_f)
    o_a, s_fin = gated_delta_recurrent(q_a, k_a, v_a, g, beta, s0.astype(jnp.float32))
    o_b = fox_sample(q_b, k_b, v_b, logf, k_past, v_past, logf_past)
    return merge_branches(o_a, o_b, gate_a, gate_b, onorm_g), (k_b, v_b, logf, s_fin, conv_new)


def trunk_layer(x, mod, norm_g, w_in, w_o, w_gu, w_down, mixer):
    sh_a, sc_a, gt_a, sh_f, sc_f, gt_f = mod
    h = rms_norm(x, norm_g[0]) * (1 + sc_a) + sh_a
    m, new_state = mixer(h @ w_in)
    x = x + gt_a * rms_norm(m @ w_o, norm_g[1])
    h = rms_norm(x, norm_g[2]) * (1 + sc_f) + sh_f
    gate, up = jnp.split(h @ w_gu, 2, axis=-1)
    x = x + gt_f * rms_norm((jax.nn.silu(gate) * up) @ w_down, norm_g[3])
    return x, new_state


def setup_inputs(seed: int = 0) -> dict:
    key = jax.random.key(seed)
    ks = jax.random.split(key, 24)
    n_pages = PAST_LEN // PAGE_SIZE
    n_pool = (DEC_BATCH * n_pages * 5) // 4

    def nrm(k, shape, s):
        return jax.random.normal(k, shape, jnp.float32) * s

    page_table = jax.random.permutation(ks[7], n_pool)[:DEC_BATCH * n_pages].reshape(DEC_BATCH, n_pages).astype(jnp.int32)
    dt = jnp.exp(jax.random.uniform(ks[16], (DEPTH, H_A), jnp.float32, np.log(1e-3), np.log(1e-1)))
    return {
        "x_prompt": nrm(ks[0], (BATCH, SEQ, D_MODEL), 1.0),
        "x_sample": nrm(ks[1], (DEC_BATCH, DEC_SEQ, D_MODEL), 1.0),
        "cache_k": nrm(ks[2], (n_pool, DEPTH, PAGE_SIZE, KV_B, HD_B), 1.0),
        "cache_v": nrm(ks[3], (n_pool, DEPTH, PAGE_SIZE, KV_B, HD_B), 1.0),
        "cache_logf": jax.nn.log_sigmoid(nrm(ks[4], (n_pool, DEPTH, PAGE_SIZE, H_B), 1.0) + FORGET_BIAS_MEAN),
        "state_delta": nrm(ks[5], (DEC_BATCH, DEPTH, H_A, HD_A, HD_A), 0.05),
        "state_conv": nrm(ks[6], (DEC_BATCH, DEPTH, CONV_W - 1, QKV_A), 1.0),
        "page_table": page_table,
        "c_prompt": nrm(ks[8], (BATCH, D_MODEL), 1.0),
        "c_sample": nrm(ks[9], (DEC_BATCH, D_MODEL), 1.0),
        "norm_g": 1.0 + nrm(ks[10], (DEPTH, 4, D_MODEL), 0.02),
        "w_ada": nrm(ks[11], (DEPTH, D_MODEL, 6 * D_MODEL), 0.5 * D_MODEL ** -0.5),
        "b_ada": nrm(ks[12], (DEPTH, 6 * D_MODEL), 0.02),
        "w_in": nrm(ks[13], (DEPTH, D_MODEL, N_IN), D_MODEL ** -0.5),
        "conv_w": nrm(ks[14], (DEPTH, CONV_W, QKV_A), CONV_W ** -0.5),
        "a_log": jnp.log(jax.random.uniform(ks[15], (DEPTH, H_A), jnp.float32, 1.0, 16.0)),
        "dt_bias": dt + jnp.log(-jnp.expm1(-dt)),
        "b_f": FORGET_BIAS_MEAN + nrm(ks[17], (DEPTH, H_B), 0.5),
        "onorm_g": 1.0 + nrm(ks[18], (DEPTH, HD_A), 0.02),
        "w_o": nrm(ks[19], (DEPTH, D_MODEL, D_MODEL), D_MODEL ** -0.5),
        "w_gu": nrm(ks[20], (DEPTH, D_MODEL, 2 * D_FF), D_MODEL ** -0.5),
        "w_down": nrm(ks[21], (DEPTH, D_FF, D_MODEL), D_FF ** -0.5),
    }


def reference(x_prompt, x_sample, cache_k, cache_v, cache_logf, state_delta, state_conv, page_table,
              c_prompt, c_sample, norm_g, w_ada, b_ada, w_in, conv_w, a_log, dt_bias, b_f, onorm_g,
              w_o, w_gu, w_down):
    n_dec = x_sample.shape[0]
    y_p, y_s = x_prompt, x_sample
    st_p = ([], [], [], [], [])
    st_s = ([], [], [], [], [])
    for l in range(DEPTH):
        mix_p = functools.partial(prompt_mixers, conv_w=conv_w[l], a_log=a_log[l], dt_bias=dt_bias[l],
                                  b_f=b_f[l], onorm_g=onorm_g[l])
        y_p, new_p = trunk_layer(y_p, adaln(c_prompt, w_ada[l], b_ada[l]), norm_g[l], w_in[l], w_o[l],
                                 w_gu[l], w_down[l], mix_p)
        k_past = cache_k[page_table, l].reshape(n_dec, -1, KV_B, HD_B)
        v_past = cache_v[page_table, l].reshape(n_dec, -1, KV_B, HD_B)
        lf_past = cache_logf[page_table, l].reshape(n_dec, -1, H_B)
        mix_s = functools.partial(sample_mixers, conv_buf=state_conv[:, l], s0=state_delta[:, l],
                                  k_past=k_past, v_past=v_past, logf_past=lf_past, conv_w=conv_w[l],
                                  a_log=a_log[l], dt_bias=dt_bias[l], b_f=b_f[l], onorm_g=onorm_g[l])
        y_s, new_s = trunk_layer(y_s, adaln(c_sample, w_ada[l], b_ada[l]), norm_g[l], w_in[l], w_o[l],
                                 w_gu[l], w_down[l], mix_s)
        for lst, val in zip(st_p, new_p):
            lst.append(val)
        for lst, val in zip(st_s, new_s):
            lst.append(val)
    k_p, v_p, lf_p, d_p, cv_p = (jnp.stack(lst, axis=1) for lst in st_p)
    k_s, v_s, lf_s, d_s, cv_s = (jnp.stack(lst, axis=1) for lst in st_s)
    return (y_p, y_s, k_p, v_p, lf_p, d_p, cv_p, k_s, v_s, lf_s, d_s, cv_s)
```

```python
import functools

import jax
import jax.numpy as jnp
from jax import lax
from jax.experimental import pallas as pl
from jax.experimental.pallas import tpu as pltpu

HD = 128
CONV_W = 4
CHUNK = 64
NORM_EPS = 1e-6
L2_EPS = 1e-6
LANES = 128
SUBLANES = 8
VMEM_LIMIT = 56 * 1024 * 1024
HI = lax.Precision.HIGHEST
F32 = jnp.float32
BF16 = jnp.bfloat16


def _cparams(n_axes):
    return pltpu.CompilerParams(dimension_semantics=("arbitrary",) * n_axes, vmem_limit_bytes=VMEM_LIMIT)


def _pick(n, pref, align):
    if n <= pref:
        return n
    t = (pref // align) * align
    while t > align and n % t:
        t -= align
    assert n % t == 0, (n, pref, align)
    return t


def _silu(x):
    return x * jax.nn.sigmoid(x)


def _softplus(x):
    return jnp.maximum(x, 0.0) + jnp.log1p(jnp.exp(-jnp.abs(x)))


def _rms(x, g):
    return x * lax.rsqrt(jnp.mean(x * x, axis=-1, keepdims=True) + NORM_EPS) * g


def _dot_nt(a, b, **kw):
    return lax.dot_general(a, b, (((1,), (1,)), ((), ())), preferred_element_type=F32, **kw)


def _dot_tn(a, b, **kw):
    return lax.dot_general(a, b, (((0,), (0,)), ((), ())), preferred_element_type=F32, **kw)


def _mm_kernel(x_ref, w_ref, o_ref, wb_ref):
    @pl.when(pl.program_id(1) == 0)
    def _():
        wb_ref[...] = w_ref[0].astype(BF16)

    o_ref[...] = jnp.dot(x_ref[...], wb_ref[...], preferred_element_type=F32).astype(o_ref.dtype)


def _mm_add_kernel(x_ref, w_ref, a_ref, o_ref, wb_ref):
    @pl.when(pl.program_id(1) == 0)
    def _():
        wb_ref[...] = w_ref[0].astype(BF16)

    o_ref[...] = (a_ref[...] + jnp.dot(x_ref[...], wb_ref[...], preferred_element_type=F32)).astype(o_ref.dtype)


def _mm_swiglu_kernel(x_ref, wg_ref, wu_ref, o_ref, wgb_ref, wub_ref):
    @pl.when(pl.program_id(1) == 0)
    def _():
        wgb_ref[...] = wg_ref[0].astype(BF16)
        wub_ref[...] = wu_ref[0].astype(BF16)

    x = x_ref[...]
    g = jnp.dot(x, wgb_ref[...], preferred_element_type=F32)
    u = jnp.dot(x, wub_ref[...], preferred_element_type=F32)
    o_ref[...] = (_silu(g) * u).astype(o_ref.dtype)


def _matmul(x, w, layer, *, col0=0, ncols=None, kblk=0, kdim=None, add=None, out_dtype=F32, tm=1024, tn=512):
    M = x.shape[0]
    kdim = x.shape[1] if kdim is None else kdim
    ncols = w.shape[2] - col0 if ncols is None else ncols
    tm = _pick(M, tm, SUBLANES)
    tn = _pick(ncols, tn, LANES)
    assert col0 % tn == 0
    cb = col0 // tn
    in_specs = [pl.BlockSpec((tm, kdim), lambda j, i: (i, kblk)),
                pl.BlockSpec((1, kdim, tn), lambda j, i: (layer, kblk, cb + j))]
    args = [x, w]
    body = _mm_kernel
    if add is not None:
        in_specs.append(pl.BlockSpec((tm, tn), lambda j, i: (i, j)))
        args.append(add)
        body = _mm_add_kernel
    return pl.pallas_call(
        body,
        grid=(ncols // tn, M // tm),
        in_specs=in_specs,
        out_specs=pl.BlockSpec((tm, tn), lambda j, i: (i, j)),
        out_shape=jax.ShapeDtypeStruct((M, ncols), out_dtype),
        scratch_shapes=[pltpu.VMEM((kdim, tn), BF16)],
        compiler_params=_cparams(2),
    )(*args)


def _matmul_swiglu(x, w_gu, layer, *, tm=1024, tn=256):
    M, K = x.shape
    F = w_gu.shape[2] // 2
    tm = _pick(M, tm, SUBLANES)
    tn = _pick(F, tn, LANES)
    nb = F // tn
    return pl.pallas_call(
        _mm_swiglu_kernel,
        grid=(nb, M // tm),
        in_specs=[pl.BlockSpec((tm, K), lambda j, i: (i, 0)),
                  pl.BlockSpec((1, K, tn), lambda j, i: (layer, 0, j)),
                  pl.BlockSpec((1, K, tn), lambda j, i: (layer, 0, nb + j))],
        out_specs=pl.BlockSpec((tm, tn), lambda j, i: (i, j)),
        out_shape=jax.ShapeDtypeStruct((M, F), BF16),
        scratch_shapes=[pltpu.VMEM((K, tn), BF16), pltpu.VMEM((K, tn), BF16)],
        compiler_params=_cparams(2),
    )(x, w_gu, w_gu)


def _adaln_kernel(c_ref, w_ref, b_ref, o_ref):
    a = _silu(c_ref[...]).astype(BF16)
    o_ref[0] = jnp.dot(a, w_ref[0].astype(BF16), preferred_element_type=F32) + b_ref[0]


def _adaln(c_all, w_ada, b_ada):
    depth, D, N = w_ada.shape
    R = c_all.shape[0]
    tn = _pick(N, 512, LANES)
    return pl.pallas_call(
        _adaln_kernel,
        grid=(depth, N // tn),
        in_specs=[pl.BlockSpec((R, D), lambda l, j: (0, 0)),
                  pl.BlockSpec((1, D, tn), lambda l, j: (l, 0, j)),
                  pl.BlockSpec((1, 1, tn), lambda l, j: (l, 0, j))],
        out_specs=pl.BlockSpec((1, R, tn), lambda l, j: (l, 0, j)),
        out_shape=jax.ShapeDtypeStruct((depth, R, N), F32),
        compiler_params=_cparams(2),
    )(c_all, w_ada, b_ada.reshape(depth, 1, N))


def _prenorm_kernel(x_ref, g_ref, sc_ref, sh_ref, h_ref):
    h_ref[0] = (_rms(x_ref[0], g_ref[...]) * (1.0 + sc_ref[0]) + sh_ref[0]).astype(h_ref.dtype)


def _resnorm_kernel(x_ref, y_ref, gpost_ref, gt_ref, gpre_ref, sc_ref, sh_ref, xo_ref, h_ref):
    xn = x_ref[0] + gt_ref[0] * _rms(y_ref[0], gpost_ref[...])
    xo_ref[0] = xn
    h_ref[0] = (_rms(xn, gpre_ref[...]) * (1.0 + sc_ref[0]) + sh_ref[0]).astype(h_ref.dtype)


def _res_kernel(x_ref, y_ref, gpost_ref, gt_ref, xo_ref):
    xo_ref[0] = x_ref[0] + gt_ref[0] * _rms(y_ref[0], gpost_ref[...])


def _row_specs(B, T, D, tl, n_mod):
    act = pl.BlockSpec((1, tl, D), lambda b, t: (b, t, 0))
    gain = pl.BlockSpec((1, D), lambda b, t: (0, 0))
    return act, gain


def _mod_spec(m, tl):
    S = m.shape[1]
    if S == 1:
        return pl.BlockSpec((1, 1, m.shape[2]), lambda b, t: (b, 0, 0))
    return pl.BlockSpec((1, tl, m.shape[2]), lambda b, t: (b, t, 0))


def _prenorm(x, g, sc, sh):
    B, T, D = x.shape
    tl = _pick(T, 256, SUBLANES)
    act, gain = _row_specs(B, T, D, tl, 2)
    return pl.pallas_call(
        _prenorm_kernel,
        grid=(B, T // tl),
        in_specs=[act, gain, _mod_spec(sc, tl), _mod_spec(sh, tl)],
        out_specs=act,
        out_shape=jax.ShapeDtypeStruct((B, T, D), BF16),
        compiler_params=_cparams(2),
    )(x, g.reshape(1, D), sc, sh)


def _resnorm(x, y, gpost, gt, gpre=None, sc=None, sh=None):
    B, T, D = x.shape
    tl = _pick(T, 256, SUBLANES)
    act, gain = _row_specs(B, T, D, tl, 3)
    if gpre is None:
        return pl.pallas_call(
            _res_kernel,
            grid=(B, T // tl),
            in_specs=[act, act, gain, _mod_spec(gt, tl)],
            out_specs=act,
            out_shape=jax.ShapeDtypeStruct((B, T, D), F32),
            compiler_params=_cparams(2),
        )(x, y, gpost.reshape(1, D), gt), None
    return pl.pallas_call(
        _resnorm_kernel,
        grid=(B, T // tl),
        in_specs=[act, act, gain, _mod_spec(gt, tl), gain, _mod_spec(sc, tl), _mod_spec(sh, tl)],
        out_specs=[act, act],
        out_shape=[jax.ShapeDtypeStruct((B, T, D), F32), jax.ShapeDtypeStruct((B, T, D), BF16)],
        compiler_params=_cparams(2),
    )(x, y, gpost.reshape(1, D), gt, gpre.reshape(1, D), sc, sh)


def _merge_kernel(oa_ref, ob_ref, ga_ref, gb_ref, m_ref):
    m = jax.nn.sigmoid(ga_ref[0]) * oa_ref[0] + jax.nn.sigmoid(gb_ref[0]) * ob_ref[0]
    m_ref[0] = m.astype(m_ref.dtype)


def _merge(o_a, o_b, zg):
    B, T, W = o_a.shape
    tl = _pick(T, 256, SUBLANES)
    tc = _pick(W, 1024, LANES)
    nc = W // tc
    spec = pl.BlockSpec((1, tl, tc), lambda b, t, c: (b, t, c))
    return pl.pallas_call(
        _merge_kernel,
        grid=(B, T // tl, nc),
        in_specs=[spec, spec, spec, pl.BlockSpec((1, tl, tc), lambda b, t, c: (b, t, nc + c))],
        out_specs=spec,
        out_shape=jax.ShapeDtypeStruct((B, T, W), BF16),
        compiler_params=_cparams(3),
    )(o_a, o_b, zg, zg)


def _gates_kernel(z_ref, prm_ref, o_ref, carry_ref, *, H, cumulative):
    z = z_ref[0]
    tl = z.shape[0]
    a_log, dt_bias, b_f = prm_ref[0:1, :], prm_ref[1:2, :], prm_ref[2:3, :]
    lane = lax.broadcasted_iota(jnp.int32, z.shape, 1)
    beta = jax.nn.sigmoid(z)
    g = -jnp.exp(a_log) * _softplus(z + dt_bias)
    lf = -_softplus(-(z + b_f))
    if cumulative:
        row = lax.broadcasted_iota(jnp.int32, (tl, tl), 0)
        col = lax.broadcasted_iota(jnp.int32, (tl, tl), 1)
        tri = row >= col
        in_chunk = tri & ((row // CHUNK) == (col // CHUNK))
        g = jnp.dot(in_chunk.astype(F32), g, precision=HI, preferred_element_type=F32)

        @pl.when(pl.program_id(1) == 0)
        def _():
            carry_ref[...] = jnp.zeros_like(carry_ref)

        c = jnp.dot(tri.astype(F32), lf, precision=HI, preferred_element_type=F32) + carry_ref[...]
        carry_ref[...] = c[tl - 1:tl, :]
    else:
        c = lf
    o_ref[0] = jnp.where(lane < H, beta, jnp.where(lane < 2 * H, g, jnp.where(lane < 3 * H, lf, c)))


def _gates(zs, prm, H, cumulative):
    B, T, _ = zs.shape
    tl = _pick(T, 256, CHUNK) if cumulative else T
    spec = pl.BlockSpec((1, tl, LANES), lambda b, t: (b, t, 0))
    return pl.pallas_call(
        functools.partial(_gates_kernel, H=H, cumulative=cumulative),
        grid=(B, T // tl),
        in_specs=[spec, pl.BlockSpec((SUBLANES, LANES), lambda b, t: (0, 0))],
        out_specs=spec,
        out_shape=jax.ShapeDtypeStruct((B, T, LANES), F32),
        scratch_shapes=[pltpu.VMEM((1, LANES), F32)],
        compiler_params=_cparams(2),
    )(zs, prm)


def _lane_col(x, idx):
    lane = lax.broadcasted_iota(jnp.int32, x.shape, 1)
    return jnp.sum(jnp.where(lane == idx, x, 0.0), axis=1, keepdims=True)


def _unit_lower_inverse(a):
    n = a.shape[0]
    row = lax.broadcasted_iota(jnp.int32, (n, n), 0)
    col = lax.broadcasted_iota(jnp.int32, (n, n), 1)
    eye = (row == col).astype(F32)
    dot = functools.partial(jnp.dot, precision=HI, preferred_element_type=F32)
    base = SUBLANES
    ad = jnp.where((row // base) == (col // base), a, 0.0)
    a2 = dot(ad, ad)
    a4 = dot(a2, a2)
    t = dot(dot(eye - ad, eye + a2), eye + a4)
    s = base
    while s < n:
        off = ((row // (2 * s)) == (col // (2 * s))) & ((row // s) != (col // s))
        t = t - dot(dot(t, jnp.where(off, a, 0.0)), t)
        s *= 2
    return t


def _conv_rows(xb, i, w_ref, n):
    acc = xb[i, SUBLANES - CONV_W + 1:SUBLANES - CONV_W + 1 + n, :] * w_ref[0, 0:1, :]
    for j in range(1, CONV_W):
        lo = SUBLANES - CONV_W + 1 + j
        acc = acc + xb[i, lo:lo + n, :] * w_ref[0, j:j + 1, :]
    return acc


def _l2n(t):
    return t * lax.rsqrt(jnp.sum(t * t, axis=-1, keepdims=True) + L2_EPS)


def _delta_prompt_kernel(zq_ref, zk_ref, zv_ref, wq_ref, wk_ref, wv_ref, gp_ref, on_ref,
                         o_ref, sfin_ref, s_ref, xb, *, H):
    h = pl.program_id(1)
    t = pl.program_id(2)
    Lb = zq_ref.shape[1]

    @pl.when(t == 0)
    def _():
        s_ref[...] = jnp.zeros_like(s_ref)
        xb[:, 0:SUBLANES, :] = jnp.zeros((3, SUBLANES, HD), F32)

    @pl.when(t > 0)
    def _():
        xb[:, 0:SUBLANES, :] = xb[:, Lb:Lb + SUBLANES, :]

    xb[0, SUBLANES:SUBLANES + Lb, :] = zq_ref[0]
    xb[1, SUBLANES:SUBLANES + Lb, :] = zk_ref[0]
    xb[2, SUBLANES:SUBLANES + Lb, :] = zv_ref[0]
    q = _l2n(_silu(_conv_rows(xb, 0, wq_ref, Lb))) * (HD ** -0.5)
    k = _l2n(_silu(_conv_rows(xb, 1, wk_ref, Lb)))
    v = _silu(_conv_rows(xb, 2, wv_ref, Lb))

    gp = gp_ref[0]
    beta = _lane_col(gp, h)
    gcum = _lane_col(gp, h + H)
    gcum_rows = jnp.transpose(jnp.broadcast_to(gcum, (Lb, LANES)))
    eg = jnp.exp(gcum)
    kb = k * beta
    rhs = jnp.concatenate([v * beta, kb * eg], axis=1)
    qg = q * eg

    row = lax.broadcasted_iota(jnp.int32, (CHUNK, CHUNK), 0)
    col = lax.broadcasted_iota(jnp.int32, (CHUNK, CHUNK), 1)
    parts = []
    for c in range(Lb // CHUNK):
        sl = slice(c * CHUNK, (c + 1) * CHUNK)
        gc = gcum[sl]
        decay = jnp.exp(jnp.where(row >= col, gc - gcum_rows[0:CHUNK, sl], -jnp.inf))
        kc = k[sl].astype(BF16)
        a_mat = jnp.where(row > col, _dot_nt(kb[sl].astype(BF16), kc) * decay, 0.0)
        a_qk = _dot_nt(q[sl].astype(BF16), kc) * decay
        sol = jnp.dot(_unit_lower_inverse(a_mat), rhs[sl], precision=HI, preferred_element_type=F32)
        g_last = gc[CHUNK - 1:CHUNK, :]
        kd = k[sl] * jnp.exp(g_last - gc)
        parts.append((sol[:, :HD], sol[:, HD:], a_qk, qg[sl], kd, jnp.exp(g_last)))

    s = s_ref[...]
    outs = []
    for u, w, a_qk, qg_c, kd, gl in parts:
        ws_qs = jnp.dot(jnp.concatenate([w, qg_c], axis=0).astype(BF16), s.astype(BF16), preferred_element_type=F32)
        delta = u - ws_qs[:CHUNK]
        db = delta.astype(BF16)
        outs.append(ws_qs[CHUNK:] + jnp.dot(a_qk.astype(BF16), db, preferred_element_type=F32))
        s = gl * s + _dot_tn(kd.astype(BF16), db)
    s_ref[...] = s
    o_ref[0] = _rms(jnp.concatenate(outs, axis=0), on_ref[...])

    @pl.when(t == pl.num_programs(2) - 1)
    def _():
        sfin_ref[0, 0] = s


def _delta_prompt(z, gp, conv_w, onorm, layer, H):
    B, L, _ = z.shape
    Lb = _pick(L, 256, CHUNK)

    def zspec(off):
        return pl.BlockSpec((1, Lb, HD), lambda b, h, t: (b, t, off + h))

    def wspec(off):
        return pl.BlockSpec((1, CONV_W, HD), lambda b, h, t: (layer, 0, off + h))

    return pl.pallas_call(
        functools.partial(_delta_prompt_kernel, H=H),
        grid=(B, H, L // Lb),
        in_specs=[zspec(0), zspec(H), zspec(2 * H), wspec(0), wspec(H), wspec(2 * H),
                  pl.BlockSpec((1, Lb, LANES), lambda b, h, t: (b, t, 0)),
                  pl.BlockSpec((1, HD), lambda b, h, t: (0, 0))],
        out_specs=[pl.BlockSpec((1, Lb, HD), lambda b, h, t: (b, t, h)),
                   pl.BlockSpec((1, 1, HD, HD), lambda b, h, t: (b, h, 0, 0))],
        out_shape=[jax.ShapeDtypeStruct((B, L, H * HD), F32), jax.ShapeDtypeStruct((B, H, HD, HD), F32)],
        scratch_shapes=[pltpu.VMEM((HD, HD), F32), pltpu.VMEM((3, Lb + SUBLANES, HD), F32)],
        compiler_params=_cparams(3),
    )(z, z, z, conv_w, conv_w, conv_w, gp, onorm)


def _delta_sample_kernel(zq_ref, zk_ref, zv_ref, bq_ref, bk_ref, bv_ref, wq_ref, wk_ref, wv_ref, gp_ref, on_ref,
                         s0_ref, o_ref, s_ref, *, H):
    h = pl.program_id(1)

    def conv(z_ref, b_ref, w_ref):
        acc = z_ref[0] * w_ref[0, CONV_W - 1:CONV_W, :]
        for j in range(CONV_W - 1):
            acc = acc + b_ref[0, 0, j:j + 1, :] * w_ref[0, j:j + 1, :]
        return acc

    q = _l2n(_silu(conv(zq_ref, bq_ref, wq_ref))) * (HD ** -0.5)
    k = _l2n(_silu(conv(zk_ref, bk_ref, wk_ref)))
    v = _silu(conv(zv_ref, bv_ref, wv_ref))
    gp = gp_ref[0]
    beta = _lane_col(gp, h)
    g = _lane_col(gp, h + H)
    dot = functools.partial(jnp.dot, precision=HI, preferred_element_type=F32)
    s = jnp.exp(g) * s0_ref[0, 0, 0]
    k8 = jnp.broadcast_to(k, (SUBLANES, HD))
    delta = beta * (v - dot(k8, s)[0:1])
    row = lax.broadcasted_iota(jnp.int32, (HD, HD), 0)
    k_row0 = jnp.where(row == 0, jnp.broadcast_to(k, (HD, HD)), 0.0)
    s = s + _dot_tn(k_row0, jnp.broadcast_to(delta, (HD, HD)), precision=HI)
    s_ref[0, 0] = s
    o = dot(jnp.broadcast_to(q, (SUBLANES, HD)), s)[0:1]
    o_ref[0] = _rms(o, on_ref[...])


def _delta_sample(z3, state_conv, state_delta, gp, conv_w, onorm, layer, H):
    B = z3.shape[0]

    def zspec(off):
        return pl.BlockSpec((1, 1, HD), lambda b, h: (b, 0, off + h))

    def bspec(off):
        return pl.BlockSpec((1, 1, CONV_W - 1, HD), lambda b, h: (b, layer, 0, off + h))

    def wspec(off):
        return pl.BlockSpec((1, CONV_W, HD), lambda b, h: (layer, 0, off + h))

    return pl.pallas_call(
        functools.partial(_delta_sample_kernel, H=H),
        grid=(B, H),
        in_specs=[zspec(0), zspec(H), zspec(2 * H), bspec(0), bspec(H), bspec(2 * H),
                  wspec(0), wspec(H), wspec(2 * H),
                  pl.BlockSpec((1, 1, LANES), lambda b, h: (b, 0, 0)),
                  pl.BlockSpec((1, HD), lambda b, h: (0, 0)),
                  pl.BlockSpec((1, 1, 1, HD, HD), lambda b, h: (b, layer, h, 0, 0))],
        out_specs=[pl.BlockSpec((1, 1, HD), lambda b, h: (b, 0, h)),
                   pl.BlockSpec((1, 1, HD, HD), lambda b, h: (b, h, 0, 0))],
        out_shape=[jax.ShapeDtypeStruct((B, 1, H * HD), F32), jax.ShapeDtypeStruct((B, H, HD, HD), F32)],
        compiler_params=_cparams(2),
    )(z3, z3, z3, state_conv, state_conv, state_conv, conv_w, conv_w, conv_w, gp, onorm, state_delta)


def _fox_prompt_kernel(q_ref, k_ref, v_ref, gp_ref, ct_ref, o_ref, m_ref, l_ref, acc_ref, cq_ref, *, H, G):
    kvh = pl.program_id(1)
    qi = pl.program_id(2)
    ki = pl.program_id(3)
    tq = q_ref.shape[1]
    tk = k_ref.shape[1]

    @pl.when(ki == 0)
    def _():
        m_ref[...] = jnp.full(m_ref.shape, -jnp.inf, F32)
        l_ref[...] = jnp.zeros_like(l_ref)
        acc_ref[...] = jnp.zeros_like(acc_ref)
        gp = gp_ref[0]
        for g in range(G):
            cq_ref[g] = _lane_col(gp, 3 * H + kvh * G + g)

    @pl.when(ki <= qi)
    def _():
        kb = k_ref[0].astype(BF16)
        vb = v_ref[0].astype(BF16)
        qpos = qi * tq + lax.broadcasted_iota(jnp.int32, (tq, tk), 0)
        kpos = ki * tk + lax.broadcasted_iota(jnp.int32, (tq, tk), 1)
        for g in range(G):
            qb = q_ref[0, :, g * HD:(g + 1) * HD].astype(BF16)
            s = _dot_nt(qb, kb) * (HD ** -0.5)
            s = s + (cq_ref[g] - ct_ref[0, pl.ds(kvh * G + g, 1), :])
            s = jnp.where(kpos <= qpos, s, -jnp.inf)
            m_old = m_ref[g]
            m_new = jnp.maximum(m_old, jnp.max(s, axis=1, keepdims=True))
            alpha = jnp.exp(m_old - m_new)
            p = jnp.exp(s - m_new)
            l_ref[g] = alpha * l_ref[g] + jnp.sum(p, axis=1, keepdims=True)
            acc_ref[g] = alpha * acc_ref[g] + jnp.dot(p.astype(BF16), vb, preferred_element_type=F32)
            m_ref[g] = m_new

    @pl.when(ki == pl.num_programs(3) - 1)
    def _():
        for g in range(G):
            o_ref[0, :, g * HD:(g + 1) * HD] = acc_ref[g] / l_ref[g]


def _fox_prompt(z, gp, ct, H, KV, q_col, k_col, v_col):
    B, L, _ = z.shape
    G = H // KV
    t = _pick(L, 512, LANES)
    n = L // t
    qw = G * HD
    return pl.pallas_call(
        functools.partial(_fox_prompt_kernel, H=H, G=G),
        grid=(B, KV, n, n),
        in_specs=[pl.BlockSpec((1, t, qw), lambda b, kv, qi, ki: (b, qi, q_col // qw + kv)),
                  pl.BlockSpec((1, t, HD), lambda b, kv, qi, ki: (b, jnp.minimum(ki, qi), k_col // HD + kv)),
                  pl.BlockSpec((1, t, HD), lambda b, kv, qi, ki: (b, jnp.minimum(ki, qi), v_col // HD + kv)),
                  pl.BlockSpec((1, t, LANES), lambda b, kv, qi, ki: (b, qi, 0)),
                  pl.BlockSpec((1, H, t), lambda b, kv, qi, ki: (b, 0, jnp.minimum(ki, qi)))],
        out_specs=pl.BlockSpec((1, t, qw), lambda b, kv, qi, ki: (b, qi, kv)),
        out_shape=jax.ShapeDtypeStruct((B, L, H * HD), F32),
        scratch_shapes=[pltpu.VMEM((G, t, 1), F32), pltpu.VMEM((G, t, 1), F32), pltpu.VMEM((G, t, HD), F32),
                        pltpu.VMEM((G, t, 1), F32)],
        compiler_params=_cparams(4),
    )(z, z, z, gp, ct)


def _fox_sample_kernel(pt_ref, q_ref, kn_ref, vn_ref, gp_ref, k_ref, v_ref, lf_ref, o_ref,
                       qbd_ref, m_ref, l_ref, acc_ref, carry_ref, *, H, G):
    p = pl.program_id(1)
    KV = H // G
    W = KV * HD
    row = lax.broadcasted_iota(jnp.int32, (H, W), 0)
    col = lax.broadcasted_iota(jnp.int32, (H, W), 1)
    own = (row // G) == (col // HD)

    @pl.when(p == 0)
    def _():
        q = q_ref[0]
        qbd_ref[...] = jnp.where(own, jnp.concatenate([q] * KV, axis=1), 0.0).astype(BF16)
        m_ref[...] = jnp.full(m_ref.shape, -jnp.inf, F32)
        l_ref[...] = jnp.zeros_like(l_ref)
        acc_ref[...] = jnp.zeros_like(acc_ref)
        r2 = lax.broadcasted_iota(jnp.int32, (H, LANES), 0)
        c2 = lax.broadcasted_iota(jnp.int32, (H, LANES), 1)
        lf_new = jnp.broadcast_to(gp_ref[0], (H, LANES))
        carry_ref[...] = jnp.sum(jnp.where(c2 == r2 + 2 * H, lf_new, 0.0), axis=1, keepdims=True)

    P = k_ref.shape[2]
    s = _dot_nt(qbd_ref[...], k_ref[0, 0].astype(BF16)) * (HD ** -0.5)
    lf = lf_ref[0, 0]
    r = lax.broadcasted_iota(jnp.int32, (P, 2 * P), 0)
    c = lax.broadcasted_iota(jnp.int32, (P, 2 * P), 1)
    later = ((r > c) | (c >= P)).astype(F32)
    sums = _dot_tn(lf, later, precision=HI)
    s = s + carry_ref[...] + sums[:, :P]
    carry_ref[...] = carry_ref[...] + sums[:, P:P + 1]
    m_old = m_ref[...]
    m_new = jnp.maximum(m_old, jnp.max(s, axis=1, keepdims=True))
    alpha = jnp.exp(m_old - m_new)
    pr = jnp.exp(s - m_new)
    l_ref[...] = alpha * l_ref[...] + jnp.sum(pr, axis=1, keepdims=True)
    acc_ref[...] = alpha * acc_ref[...] + jnp.dot(pr.astype(BF16), v_ref[0, 0].astype(BF16), preferred_element_type=F32)
    m_ref[...] = m_new

    @pl.when(p == pl.num_programs(1) - 1)
    def _():
        s_new = jnp.sum(qbd_ref[...].astype(F32) * kn_ref[0], axis=1, keepdims=True) * (HD ** -0.5)
        m_old = m_ref[...]
        m_new = jnp.maximum(m_old, s_new)
        alpha = jnp.exp(m_old - m_new)
        p_new = jnp.exp(s_new - m_new)
        l_fin = alpha * l_ref[...] + p_new
        acc = (alpha * acc_ref[...] + p_new * vn_ref[0]) / l_fin
        acc = jnp.where(own, acc, 0.0)
        out = acc[:, 0:HD]
        for kv in range(1, KV):
            out = out + acc[:, kv * HD:(kv + 1) * HD]
        o_ref[0] = out


def _fox_sample(q, k_new, v_new, gp, cache_k, cache_v, cache_logf, page_table, layer, H, KV):
    B, n_pages = page_table.shape
    P = cache_k.shape[2]
    W = KV * HD
    G = H // KV

    def page(b, p, pt):
        return pt[b, n_pages - 1 - p]

    grid_spec = pltpu.PrefetchScalarGridSpec(
        num_scalar_prefetch=1,
        grid=(B, n_pages),
        in_specs=[pl.BlockSpec((1, H, HD), lambda b, p, pt: (b, 0, 0)),
                  pl.BlockSpec((1, 1, W), lambda b, p, pt: (b, 0, 0)),
                  pl.BlockSpec((1, 1, W), lambda b, p, pt: (b, 0, 0)),
                  pl.BlockSpec((1, 1, LANES), lambda b, p, pt: (b, 0, 0)),
                  pl.BlockSpec((1, 1, P, W), lambda b, p, pt: (page(b, p, pt), layer, 0, 0)),
                  pl.BlockSpec((1, 1, P, W), lambda b, p, pt: (page(b, p, pt), layer, 0, 0)),
                  pl.BlockSpec((1, 1, P, H), lambda b, p, pt: (page(b, p, pt), layer, 0, 0))],
        out_specs=pl.BlockSpec((1, H, HD), lambda b, p, pt: (b, 0, 0)),
        scratch_shapes=[pltpu.VMEM((H, W), BF16), pltpu.VMEM((H, 1), F32), pltpu.VMEM((H, 1), F32),
                        pltpu.VMEM((H, W), F32), pltpu.VMEM((H, 1), F32)],
    )
    return pl.pallas_call(
        functools.partial(_fox_sample_kernel, H=H, G=G),
        grid_spec=grid_spec,
        out_shape=jax.ShapeDtypeStruct((B, H, HD), F32),
        compiler_params=_cparams(2),
    )(page_table, q, k_new, v_new, gp, cache_k, cache_v, cache_logf)


def _split6(m):
    return tuple(jnp.split(m, 6, axis=-1))


def kernel(x_prompt, x_sample, cache_k, cache_v, cache_logf, state_delta, state_conv, page_table, c_prompt, c_sample,
           norm_g, w_ada, b_ada, w_in, conv_w, a_log, dt_bias, b_f, onorm_g, w_o, w_gu, w_down):
    B, L, D = x_prompt.shape
    BS = x_sample.shape[0]
    depth = w_in.shape[0]
    H = D // HD
    KV = cache_k.shape[3]
    KVW = KV * HD
    F = w_down.shape[1]
    c_main = 3 * D + D + 2 * KVW
    c_gate = c_main + 3 * H
    q_col, k_col, v_col = 3 * D, 4 * D, 4 * D + KVW
    assert 4 * H <= LANES and w_in.shape[2] == c_gate + 2 * D

    n_c = B + BS
    pad = (-n_c) % SUBLANES
    c_all = jnp.concatenate([c_prompt, c_sample, jnp.zeros((pad, D), F32)], axis=0)
    mod = _adaln(c_all, w_ada, b_ada)

    w_small = jnp.concatenate([w_in[:, :, c_main:c_gate], w_in[:, :, c_main + 2 * H:c_gate],
                               jnp.zeros((depth, D, LANES - 4 * H), F32)], axis=2)
    w_gate = w_in[:, :, c_gate:]
    zl = jnp.zeros((depth, H), F32)
    prm = jnp.stack([jnp.concatenate([zl, a_log, zl, zl], 1), jnp.concatenate([zl, dt_bias, zl, zl], 1),
                     jnp.concatenate([zl, zl, b_f, b_f], 1)], axis=1)
    prm = jnp.pad(prm, ((0, 0), (0, SUBLANES - 3), (0, LANES - 4 * H)))

    cache_k2 = cache_k.reshape(cache_k.shape[0], depth, cache_k.shape[2], KVW)
    cache_v2 = cache_v.reshape(cache_k2.shape)

    xp, xs = x_prompt, x_sample.reshape(1, BS, D)
    st_p = ([], [], [], [], [])
    st_s = ([], [], [], [], [])
    hp = hs = None
    for l in range(depth):
        mp = [m.reshape(B, 1, D) for m in _split6(mod[l, :B])]
        ms = [m.reshape(1, BS, D) for m in _split6(mod[l, B:n_c])]
        if l == 0:
            hp = _prenorm(xp, norm_g[l, 0], mp[1], mp[0])
            hs = _prenorm(xs, norm_g[l, 0], ms[1], ms[0])
        onorm = onorm_g[l].reshape(1, HD)

        h2 = hp.reshape(B * L, D)
        z = _matmul(h2, w_in, l, ncols=c_main).reshape(B, L, c_main)
        zs = _matmul(h2, w_small, l, tn=LANES).reshape(B, L, LANES)
        zg = _matmul(h2, w_gate, l).reshape(B, L, 2 * D)
        gp = _gates(zs, prm[l], H, True)
        o_a, s_fin = _delta_prompt(z, gp, conv_w, onorm, l, H)
        ct = jnp.swapaxes(gp[:, :, 3 * H:4 * H], 1, 2)
        o_b = _fox_prompt(z, gp, ct, H, KV, q_col, k_col, v_col)
        m = _merge(o_a, o_b, zg).reshape(B * L, D)
        y = _matmul(m, w_o, l).reshape(B, L, D)
        xp, hf = _resnorm(xp, y, norm_g[l, 1], mp[2], norm_g[l, 2], mp[4], mp[3])
        act = _matmul_swiglu(hf.reshape(B * L, D), w_gu, l)
        half = F // 2 if (F // 2) % LANES == 0 else F
        y = None
        for kb in range(F // half):
            y = _matmul(act, w_down, l, kblk=kb, kdim=half, add=y, tn=256)
        y = y.reshape(B, L, D)
        if l + 1 < depth:
            mnext = [mm.reshape(B, 1, D) for mm in _split6(mod[l + 1, :B])]
            xp, hp = _resnorm(xp, y, norm_g[l, 3], mp[5], norm_g[l + 1, 0], mnext[1], mnext[0])
        else:
            xp, _ = _resnorm(xp, y, norm_g[l, 3], mp[5])
        st_p[0].append(z[:, :, k_col:k_col + KVW].reshape(B, L, KV, HD))
        st_p[1].append(z[:, :, v_col:v_col + KVW].reshape(B, L, KV, HD))
        st_p[2].append(gp[:, :, 2 * H:3 * H])
        st_p[3].append(s_fin)
        st_p[4].append(z[:, L - (CONV_W - 1):, :3 * D])

        h2 = hs.reshape(BS, D)
        z = _matmul(h2, w_in, l, ncols=c_main)
        zs = _matmul(h2, w_small, l, tn=LANES).reshape(BS, 1, LANES)
        zg = _matmul(h2, w_gate, l).reshape(1, BS, 2 * D)
        gp = _gates(zs, prm[l], H, False)
        z3 = z.reshape(BS, 1, c_main)
        o_a, s_new = _delta_sample(z3, state_conv, state_delta, gp, conv_w, onorm, l, H)
        o_b = _fox_sample(z[:, q_col:q_col + D].reshape(BS, H, HD), z3[:, :, k_col:k_col + KVW],
                          z3[:, :, v_col:v_col + KVW], gp, cache_k2, cache_v2, cache_logf, page_table, l, H, KV)
        m = _merge(o_a.reshape(1, BS, D), o_b.reshape(1, BS, D), zg).reshape(BS, D)
        y = _matmul(m, w_o, l).reshape(1, BS, D)
        xs, hf = _resnorm(xs, y, norm_g[l, 1], ms[2], norm_g[l, 2], ms[4], ms[3])
        act = _matmul_swiglu(hf.reshape(BS, D), w_gu, l)
        y = None
        for kb in range(F // half):
            y = _matmul(act, w_down, l, kblk=kb, kdim=half, add=y, tn=256)
        y = y.reshape(1, BS, D)
        if l + 1 < depth:
            mnext = [mm.reshape(1, BS, D) for mm in _split6(mod[l + 1, B:n_c])]
            xs, hs = _resnorm(xs, y, norm_g[l, 3], ms[5], norm_g[l + 1, 0], mnext[1], mnext[0])
        else:
            xs, _ = _resnorm(xs, y, norm_g[l, 3], ms[5])
        st_s[0].append(z[:, k_col:k_col + KVW].reshape(BS, 1, KV, HD))
        st_s[1].append(z[:, v_col:v_col + KVW].reshape(BS, 1, KV, HD))
        st_s[2].append(gp[:, :, 2 * H:3 * H])
        st_s[3].append(s_new)
        st_s[4].append(jnp.concatenate([state_conv[:, l, 1:], z3[:, :, :3 * D]], axis=1))

    k_p, v_p, lf_p, d_p, cv_p = (jnp.stack(lst, axis=1) for lst in st_p)
    k_s, v_s, lf_s, d_s, cv_s = (jnp.stack(lst, axis=1) for lst in st_s)
    return (xp, xs.reshape(BS, 1, D), k_p, v_p, lf_p, d_p, cv_p, k_s, v_s, lf_s, d_s, cv_s)
```

```python
import functools

import jax
import jax.numpy as jnp
from jax import lax
from jax.experimental import pallas as pl
from jax.experimental.pallas import tpu as pltpu

HD = 128
CONV_W = 4
CHUNK = 64
NORM_EPS = 1e-6
L2_EPS = 1e-6
LOG2E = 1.4426950408889634
LANES = 128
SUBLANES = 8
VMEM_LIMIT = 56 * 1024 * 1024
HI = lax.Precision.HIGHEST
F32 = jnp.float32
BF16 = jnp.bfloat16


def _cparams(n_axes):
    return pltpu.CompilerParams(dimension_semantics=("arbitrary",) * n_axes, vmem_limit_bytes=VMEM_LIMIT)


def _pick(n, pref, align):
    if n <= pref:
        return n
    t = (pref // align) * align
    while t > align and n % t:
        t -= align
    assert n % t == 0, (n, pref, align)
    return t


def _silu(x):
    return x * jax.nn.sigmoid(x)


def _softplus(x):
    return jnp.maximum(x, 0.0) + jnp.log1p(jnp.exp(-jnp.abs(x)))


def _rms(x, g):
    return x * lax.rsqrt(jnp.mean(x * x, axis=-1, keepdims=True) + NORM_EPS) * g


def _dot_nt(a, b, **kw):
    return lax.dot_general(a, b, (((1,), (1,)), ((), ())), preferred_element_type=F32, **kw)


def _dot_tn(a, b, **kw):
    return lax.dot_general(a, b, (((0,), (0,)), ((), ())), preferred_element_type=F32, **kw)


def _mm_kernel(x_ref, w_ref, o_ref, wb_ref):
    @pl.when(pl.program_id(1) == 0)
    def _():
        wb_ref[...] = w_ref[0].astype(BF16)

    o_ref[...] = jnp.dot(x_ref[...], wb_ref[...], preferred_element_type=F32).astype(o_ref.dtype)


def _mm_add_kernel(x_ref, w_ref, a_ref, o_ref, wb_ref):
    @pl.when(pl.program_id(1) == 0)
    def _():
        wb_ref[...] = w_ref[0].astype(BF16)

    o_ref[...] = (a_ref[...] + jnp.dot(x_ref[...], wb_ref[...], preferred_element_type=F32)).astype(o_ref.dtype)


def _mm_swiglu_kernel(x_ref, wg_ref, wu_ref, o_ref, wgb_ref, wub_ref):
    @pl.when(pl.program_id(1) == 0)
    def _():
        wgb_ref[...] = wg_ref[0].astype(BF16)
        wub_ref[...] = wu_ref[0].astype(BF16)

    x = x_ref[...]
    g = jnp.dot(x, wgb_ref[...], preferred_element_type=F32)
    u = jnp.dot(x, wub_ref[...], preferred_element_type=F32)
    o_ref[...] = (_silu(g) * u).astype(o_ref.dtype)


def _matmul(x, w, layer, *, col0=0, ncols=None, kblk=0, kdim=None, add=None, out_dtype=F32, tm=1024, tn=512):
    M = x.shape[0]
    kdim = x.shape[1] if kdim is None else kdim
    ncols = w.shape[2] - col0 if ncols is None else ncols
    tm = _pick(M, tm, SUBLANES)
    tn = _pick(ncols, tn, LANES)
    assert col0 % tn == 0
    cb = col0 // tn
    in_specs = [pl.BlockSpec((tm, kdim), lambda j, i: (i, kblk)),
                pl.BlockSpec((1, kdim, tn), lambda j, i: (layer, kblk, cb + j))]
    args = [x, w]
    body = _mm_kernel
    if add is not None:
        in_specs.append(pl.BlockSpec((tm, tn), lambda j, i: (i, j)))
        args.append(add)
        body = _mm_add_kernel
    return pl.pallas_call(
        body,
        grid=(ncols // tn, M // tm),
        in_specs=in_specs,
        out_specs=pl.BlockSpec((tm, tn), lambda j, i: (i, j)),
        out_shape=jax.ShapeDtypeStruct((M, ncols), out_dtype),
        scratch_shapes=[pltpu.VMEM((kdim, tn), BF16)],
        compiler_params=_cparams(2),
    )(*args)


def _mm_nt_kernel(x_ref, w_ref, o_ref, wb_ref):
    @pl.when(pl.program_id(1) == 0)
    def _():
        wb_ref[...] = w_ref[0].astype(BF16)

    o_ref[...] = _dot_nt(x_ref[...], wb_ref[...]).astype(o_ref.dtype)


def _matmul_nt(x, w_t, layer, *, row0, nrows, out_dtype=F32, tm=1024, tn=512):
    M, K = x.shape
    tm = _pick(M, tm, SUBLANES)
    tn = _pick(nrows, tn, LANES)
    if row0 % tn == 0:
        w_spec = pl.BlockSpec((1, tn, K), lambda j, i: (layer, row0 // tn + j, 0))
    else:
        assert row0 % SUBLANES == 0
        w_spec = pl.BlockSpec((pl.Element(1), pl.Element(tn), pl.Element(K)),
                              lambda j, i: (layer, pl.multiple_of(row0 + j * tn, SUBLANES), 0))
    return pl.pallas_call(
        _mm_nt_kernel,
        grid=(nrows // tn, M // tm),
        in_specs=[pl.BlockSpec((tm, K), lambda j, i: (i, 0)), w_spec],
        out_specs=pl.BlockSpec((tm, tn), lambda j, i: (i, j)),
        out_shape=jax.ShapeDtypeStruct((M, nrows), out_dtype),
        scratch_shapes=[pltpu.VMEM((tn, K), BF16)],
        compiler_params=_cparams(2),
        name="mm_nt",
    )(x, w_t)


def _matmul_swiglu(x, w_gu, layer, *, tm=1024, tn=256):
    M, K = x.shape
    F = w_gu.shape[2] // 2
    tm = _pick(M, tm, SUBLANES)
    tn = _pick(F, tn, LANES)
    nb = F // tn
    return pl.pallas_call(
        _mm_swiglu_kernel,
        grid=(nb, M // tm),
        in_specs=[pl.BlockSpec((tm, K), lambda j, i: (i, 0)),
                  pl.BlockSpec((1, K, tn), lambda j, i: (layer, 0, j)),
                  pl.BlockSpec((1, K, tn), lambda j, i: (layer, 0, nb + j))],
        out_specs=pl.BlockSpec((tm, tn), lambda j, i: (i, j)),
        out_shape=jax.ShapeDtypeStruct((M, F), BF16),
        scratch_shapes=[pltpu.VMEM((K, tn), BF16), pltpu.VMEM((K, tn), BF16)],
        compiler_params=_cparams(2),
    )(x, w_gu, w_gu)


def _adaln_kernel(c_ref, w_ref, b_ref, o_ref):
    a = _silu(c_ref[...]).astype(BF16)
    o_ref[0] = jnp.dot(a, w_ref[0].astype(BF16), preferred_element_type=F32) + b_ref[0]


def _adaln(c_all, w_ada, b_ada):
    depth, D, N = w_ada.shape
    R = c_all.shape[0]
    tn = _pick(N, 512, LANES)
    return pl.pallas_call(
        _adaln_kernel,
        grid=(depth, N // tn),
        in_specs=[pl.BlockSpec((R, D), lambda l, j: (0, 0)),
                  pl.BlockSpec((1, D, tn), lambda l, j: (l, 0, j)),
                  pl.BlockSpec((1, 1, tn), lambda l, j: (l, 0, j))],
        out_specs=pl.BlockSpec((1, R, tn), lambda l, j: (l, 0, j)),
        out_shape=jax.ShapeDtypeStruct((depth, R, N), F32),
        compiler_params=_cparams(2),
    )(c_all, w_ada, b_ada.reshape(depth, 1, N))


def _prenorm_kernel(x_ref, g_ref, sc_ref, sh_ref, h_ref):
    h_ref[0] = (_rms(x_ref[0], g_ref[...]) * (1.0 + sc_ref[0]) + sh_ref[0]).astype(h_ref.dtype)


def _resnorm_kernel(x_ref, y_ref, gpost_ref, gt_ref, gpre_ref, sc_ref, sh_ref, xo_ref, h_ref):
    xn = x_ref[0] + gt_ref[0] * _rms(y_ref[0], gpost_ref[...])
    xo_ref[0] = xn
    h_ref[0] = (_rms(xn, gpre_ref[...]) * (1.0 + sc_ref[0]) + sh_ref[0]).astype(h_ref.dtype)


def _res_kernel(x_ref, y_ref, gpost_ref, gt_ref, xo_ref):
    xo_ref[0] = x_ref[0] + gt_ref[0] * _rms(y_ref[0], gpost_ref[...])


def _row_specs(B, T, D, tl, n_mod):
    act = pl.BlockSpec((1, tl, D), lambda b, t: (b, t, 0))
    gain = pl.BlockSpec((1, D), lambda b, t: (0, 0))
    return act, gain


def _mod_spec(m, tl):
    S = m.shape[1]
    if S == 1:
        return pl.BlockSpec((1, 1, m.shape[2]), lambda b, t: (b, 0, 0))
    return pl.BlockSpec((1, tl, m.shape[2]), lambda b, t: (b, t, 0))


def _prenorm(x, g, sc, sh):
    B, T, D = x.shape
    tl = _pick(T, 256, SUBLANES)
    act, gain = _row_specs(B, T, D, tl, 2)
    return pl.pallas_call(
        _prenorm_kernel,
        grid=(B, T // tl),
        in_specs=[act, gain, _mod_spec(sc, tl), _mod_spec(sh, tl)],
        out_specs=act,
        out_shape=jax.ShapeDtypeStruct((B, T, D), BF16),
        compiler_params=_cparams(2),
    )(x, g.reshape(1, D), sc, sh)


def _resnorm(x, y, gpost, gt, gpre=None, sc=None, sh=None):
    B, T, D = x.shape
    tl = _pick(T, 256, SUBLANES)
    act, gain = _row_specs(B, T, D, tl, 3)
    if gpre is None:
        return pl.pallas_call(
            _res_kernel,
            grid=(B, T // tl),
            in_specs=[act, act, gain, _mod_spec(gt, tl)],
            out_specs=act,
            out_shape=jax.ShapeDtypeStruct((B, T, D), F32),
            compiler_params=_cparams(2),
        )(x, y, gpost.reshape(1, D), gt), None
    return pl.pallas_call(
        _resnorm_kernel,
        grid=(B, T // tl),
        in_specs=[act, act, gain, _mod_spec(gt, tl), gain, _mod_spec(sc, tl), _mod_spec(sh, tl)],
        out_specs=[act, act],
        out_shape=[jax.ShapeDtypeStruct((B, T, D), F32), jax.ShapeDtypeStruct((B, T, D), BF16)],
        compiler_params=_cparams(2),
    )(x, y, gpost.reshape(1, D), gt, gpre.reshape(1, D), sc, sh)


def _merge_kernel(oa_ref, ob_ref, ga_ref, gb_ref, m_ref):
    m = jax.nn.sigmoid(ga_ref[0]) * oa_ref[0] + jax.nn.sigmoid(gb_ref[0]) * ob_ref[0]
    m_ref[0] = m.astype(m_ref.dtype)


def _merge(o_a, o_b, zg):
    B, T, W = o_a.shape
    tl = _pick(T, 256, SUBLANES)
    tc = _pick(W, 1024, LANES)
    nc = W // tc
    spec = pl.BlockSpec((1, tl, tc), lambda b, t, c: (b, t, c))
    return pl.pallas_call(
        _merge_kernel,
        grid=(B, T // tl, nc),
        in_specs=[spec, spec, spec, pl.BlockSpec((1, tl, tc), lambda b, t, c: (b, t, nc + c))],
        out_specs=spec,
        out_shape=jax.ShapeDtypeStruct((B, T, W), BF16),
        compiler_params=_cparams(3),
    )(o_a, o_b, zg, zg)


def _gates_kernel(z_ref, prm_ref, o_ref, carry_ref, *, H, cumulative):
    z = z_ref[0]
    tl = z.shape[0]
    a_log, dt_bias, b_f = prm_ref[0:1, :], prm_ref[1:2, :], prm_ref[2:3, :]
    lane = lax.broadcasted_iota(jnp.int32, z.shape, 1)
    beta = jax.nn.sigmoid(z)
    g = -jnp.exp(a_log) * _softplus(z + dt_bias)
    lf = -_softplus(-(z + b_f))
    if cumulative:
        row = lax.broadcasted_iota(jnp.int32, (tl, tl), 0)
        col = lax.broadcasted_iota(jnp.int32, (tl, tl), 1)
        tri = row >= col
        in_chunk = tri & ((row // CHUNK) == (col // CHUNK))
        g = jnp.dot(in_chunk.astype(F32), g, precision=HI, preferred_element_type=F32)

        @pl.when(pl.program_id(1) == 0)
        def _():
            carry_ref[...] = jnp.zeros_like(carry_ref)

        c = jnp.dot(tri.astype(F32), lf, precision=HI, preferred_element_type=F32) + carry_ref[...]
        carry_ref[...] = c[tl - 1:tl, :]
    else:
        c = lf
    c = pltpu.roll(c, H, axis=1)
    o_ref[0] = jnp.where(lane < H, beta, jnp.where(lane < 2 * H, g, jnp.where(lane < 3 * H, lf, c)))


def _gates(zs, prm, H, cumulative):
    B, T, _ = zs.shape
    tl = _pick(T, 256, CHUNK) if cumulative else T
    spec = pl.BlockSpec((1, tl, LANES), lambda b, t: (b, t, 0))
    return pl.pallas_call(
        functools.partial(_gates_kernel, H=H, cumulative=cumulative),
        grid=(B, T // tl),
        in_specs=[spec, pl.BlockSpec((SUBLANES, LANES), lambda b, t: (0, 0))],
        out_specs=spec,
        out_shape=jax.ShapeDtypeStruct((B, T, LANES), F32),
        scratch_shapes=[pltpu.VMEM((1, LANES), F32)],
        compiler_params=_cparams(2),
    )(zs, prm)


def _lane_col(x, idx):
    lane = lax.broadcasted_iota(jnp.int32, x.shape, 1)
    return jnp.sum(jnp.where(lane == idx, x, 0.0), axis=1, keepdims=True)


def _bf16_dot(a, b):
    return jnp.dot(a.astype(BF16), b.astype(BF16), preferred_element_type=F32)


def _split_bf16(x):
    hi = x.astype(BF16)
    return hi, (x - hi.astype(F32)).astype(BF16)


def _dot_3pass(a, b):
    a_hi, a_lo = _split_bf16(a)
    b_hi, b_lo = _split_bf16(b)
    dot = functools.partial(jnp.dot, preferred_element_type=F32)
    return dot(a_hi, b_hi) + (dot(a_hi, b_lo) + dot(a_lo, b_hi))


def _each(f, *lists):
    return [f(*args) for args in zip(*lists)]


def _unit_lower_inverse(a_list, dot):
    n = a_list[0].shape[0]
    row = lax.broadcasted_iota(jnp.int32, (n, n), 0)
    col = lax.broadcasted_iota(jnp.int32, (n, n), 1)
    eye = (row == col).astype(F32)
    base = SUBLANES
    diag = (row // base) == (col // base)
    ad = _each(lambda a: jnp.where(diag, a, 0.0), a_list)
    a2 = _each(lambda x: dot(x, x), ad)
    a4 = _each(lambda x: dot(x, x), a2)
    t = _each(lambda x, y: dot(eye - x, eye + y), ad, a2)
    t = _each(lambda x, y: dot(x, eye + y), t, a4)
    s = base
    while s < n:
        off = ((row // (2 * s)) == (col // (2 * s))) & ((row // s) != (col // s))
        tm = _each(lambda x, a: dot(x, jnp.where(off, a, 0.0)), t, a_list)
        t = _each(lambda x, y: x - dot(y, x), t, tm)
        s *= 2
    return t


def _unit_lower_solve(a_list, rhs_list):
    t0 = _each(lambda t: t.astype(BF16), _unit_lower_inverse(a_list, _bf16_dot))
    x0 = _each(lambda t, b: jnp.dot(t, b.astype(BF16), preferred_element_type=F32), t0, rhs_list)
    r = _each(lambda a, b, x: b - x - _dot_3pass(a, x), a_list, rhs_list, x0)
    return _each(lambda t, x, rr: x + jnp.dot(t, rr.astype(BF16), preferred_element_type=F32), t0, x0, r)


def _conv_rows(xb, i, w_ref, n):
    acc = xb[i, SUBLANES - CONV_W + 1:SUBLANES - CONV_W + 1 + n, :] * w_ref[0, 0:1, :]
    for j in range(1, CONV_W):
        lo = SUBLANES - CONV_W + 1 + j
        acc = acc + xb[i, lo:lo + n, :] * w_ref[0, j:j + 1, :]
    return acc


def _l2n(t):
    return t * lax.rsqrt(jnp.sum(t * t, axis=-1, keepdims=True) + L2_EPS)


def _delta_prompt_kernel(zq_ref, zk_ref, zv_ref, wq_ref, wk_ref, wv_ref, gp_ref, gpt_ref, on_ref,
                         o_ref, sfin_ref, s_ref, xb, *, H, HB):
    hg = pl.program_id(1)
    t = pl.program_id(2)
    Lb = zq_ref.shape[1]

    @pl.when(t == 0)
    def _():
        s_ref[...] = jnp.zeros_like(s_ref)
        xb[:, 0:SUBLANES, :] = jnp.zeros((3, SUBLANES, HB * HD), F32)

    @pl.when(t > 0)
    def _():
        xb[:, 0:SUBLANES, :] = xb[:, Lb:Lb + SUBLANES, :]

    xb[0, SUBLANES:SUBLANES + Lb, :] = zq_ref[0]
    xb[1, SUBLANES:SUBLANES + Lb, :] = zk_ref[0]
    xb[2, SUBLANES:SUBLANES + Lb, :] = zv_ref[0]
    q_all = _silu(_conv_rows(xb, 0, wq_ref, Lb))
    k_all = _silu(_conv_rows(xb, 1, wk_ref, Lb))
    v_all = _silu(_conv_rows(xb, 2, wv_ref, Lb))

    gp = gp_ref[0]
    row = lax.broadcasted_iota(jnp.int32, (CHUNK, CHUNK), 0)
    col = lax.broadcasted_iota(jnp.int32, (CHUNK, CHUNK), 1)
    nc = Lb // CHUNK
    decay, kq_lhs, k_rhs, rhs_l, qg_l, kd_l, gl_l = [], [], [], [], [], [], []
    for j in range(HB):
        head = hg * HB + j
        hs = slice(j * HD, (j + 1) * HD)
        q = _l2n(q_all[:, hs]) * (HD ** -0.5)
        k = _l2n(k_all[:, hs])
        v = v_all[:, hs]
        beta = _lane_col(gp, head)
        gcum = _lane_col(gp, head + H)
        grow = gpt_ref[0, pl.ds(head + H, 1), :]
        eg = jnp.exp(gcum)
        kb = k * beta
        rhs = jnp.concatenate([v * beta, kb * eg], axis=1)
        qg = q * eg
        for c in range(nc):
            sl = slice(c * CHUNK, (c + 1) * CHUNK)
            gc = gcum[sl]
            g_last = gc[CHUNK - 1:CHUNK, :]
            decay.append(jnp.exp(jnp.where(row >= col, gc - grow[:, sl], -jnp.inf)))
            kq_lhs.append(jnp.concatenate([kb[sl], q[sl]], axis=0).astype(BF16))
            k_rhs.append(k[sl].astype(BF16))
            rhs_l.append(rhs[sl])
            qg_l.append(qg[sl])
            kd_l.append((k[sl] * jnp.exp(g_last - gc)).astype(BF16))
            gl_l.append(jnp.exp(g_last))
    kq = _each(_dot_nt, kq_lhs, k_rhs)
    a_mat = _each(lambda x, d: jnp.where(row > col, x[:CHUNK] * d, 0.0), kq, decay)
    a_qk = _each(lambda x, d: (x[CHUNK:] * d).astype(BF16), kq, decay)
    sol = _unit_lower_solve(a_mat, rhs_l)
    wq = _each(lambda x, g: jnp.concatenate([x[:, HD:], g], axis=0).astype(BF16), sol, qg_l)

    s = [s_ref[j] for j in range(HB)]
    outs = [[] for _ in range(HB)]
    for c in range(nc):
        idx = [j * nc + c for j in range(HB)]
        ws_qs = [jnp.dot(wq[i], s[j].astype(BF16), preferred_element_type=F32) for j, i in enumerate(idx)]
        db = [(sol[i][:, :HD] - ws_qs[j][:CHUNK]).astype(BF16) for j, i in enumerate(idx)]
        for j, i in enumerate(idx):
            outs[j].append(ws_qs[j][CHUNK:] + jnp.dot(a_qk[i], db[j], preferred_element_type=F32))
        s = [gl_l[i] * s[j] + _dot_tn(kd_l[i], db[j]) for j, i in enumerate(idx)]
    for j in range(HB):
        s_ref[j] = s[j]
        o_ref[0, :, j * HD:(j + 1) * HD] = _rms(jnp.concatenate(outs[j], axis=0), on_ref[...])

    @pl.when(t == pl.num_programs(2) - 1)
    def _():
        sfin_ref[0] = s_ref[...]


def _delta_prompt(z, gp, gpt, conv_w, onorm, layer, H, HB=4):
    B, L, _ = z.shape
    Lb = _pick(L, 256, CHUNK)
    HB = HB if H % HB == 0 else 1
    ng = H // HB

    def zspec(off):
        return pl.BlockSpec((1, Lb, HB * HD), lambda b, h, t: (b, t, off + h))

    def wspec(off):
        return pl.BlockSpec((1, CONV_W, HB * HD), lambda b, h, t: (layer, 0, off + h))

    return pl.pallas_call(
        functools.partial(_delta_prompt_kernel, H=H, HB=HB),
        grid=(B, ng, L // Lb),
        in_specs=[zspec(0), zspec(ng), zspec(2 * ng), wspec(0), wspec(ng), wspec(2 * ng),
                  pl.BlockSpec((1, Lb, LANES), lambda b, h, t: (b, t, 0)),
                  pl.BlockSpec((1, LANES, Lb), lambda b, h, t: (b, 0, t)),
                  pl.BlockSpec((1, HD), lambda b, h, t: (0, 0))],
        out_specs=[pl.BlockSpec((1, Lb, HB * HD), lambda b, h, t: (b, t, h)),
                   pl.BlockSpec((1, HB, HD, HD), lambda b, h, t: (b, h, 0, 0))],
        out_shape=[jax.ShapeDtypeStruct((B, L, H * HD), F32), jax.ShapeDtypeStruct((B, H, HD, HD), F32)],
        scratch_shapes=[pltpu.VMEM((HB, HD, HD), F32), pltpu.VMEM((3, Lb + SUBLANES, HB * HD), F32)],
        compiler_params=_cparams(3),
        name="delta_prompt",
    )(z, z, z, conv_w, conv_w, conv_w, gp, gpt, onorm)


def _delta_sample_kernel(zq_ref, zk_ref, zv_ref, bq_ref, bk_ref, bv_ref, wq_ref, wk_ref, wv_ref, gp_ref, on_ref,
                         s0_ref, o_ref, s_ref, *, H):
    h = pl.program_id(1)

    def conv(z_ref, b_ref, w_ref):
        acc = z_ref[0] * w_ref[0, CONV_W - 1:CONV_W, :]
        for j in range(CONV_W - 1):
            acc = acc + b_ref[0, 0, j:j + 1, :] * w_ref[0, j:j + 1, :]
        return acc

    q = _l2n(_silu(conv(zq_ref, bq_ref, wq_ref))) * (HD ** -0.5)
    k = _l2n(_silu(conv(zk_ref, bk_ref, wk_ref)))
    v = _silu(conv(zv_ref, bv_ref, wv_ref))
    gp = gp_ref[0]
    beta = _lane_col(gp, h)
    g = _lane_col(gp, h + H)
    dot = functools.partial(jnp.dot, precision=HI, preferred_element_type=F32)
    s = jnp.exp(g) * s0_ref[0, 0, 0]
    k8 = jnp.broadcast_to(k, (SUBLANES, HD))
    delta = beta * (v - dot(k8, s)[0:1])
    row = lax.broadcasted_iota(jnp.int32, (HD, HD), 0)
    k_row0 = jnp.where(row == 0, jnp.broadcast_to(k, (HD, HD)), 0.0)
    s = s + _dot_tn(k_row0, jnp.broadcast_to(delta, (HD, HD)), precision=HI)
    s_ref[0, 0] = s
    o = dot(jnp.broadcast_to(q, (SUBLANES, HD)), s)[0:1]
    o_ref[0] = _rms(o, on_ref[...])


def _delta_sample(z3, state_conv, state_delta, gp, conv_w, onorm, layer, H):
    B = z3.shape[0]

    def zspec(off):
        return pl.BlockSpec((1, 1, HD), lambda b, h: (b, 0, off + h))

    def bspec(off):
        return pl.BlockSpec((1, 1, CONV_W - 1, HD), lambda b, h: (b, layer, 0, off + h))

    def wspec(off):
        return pl.BlockSpec((1, CONV_W, HD), lambda b, h: (layer, 0, off + h))

    return pl.pallas_call(
        functools.partial(_delta_sample_kernel, H=H),
        grid=(B, H),
        in_specs=[zspec(0), zspec(H), zspec(2 * H), bspec(0), bspec(H), bspec(2 * H),
                  wspec(0), wspec(H), wspec(2 * H),
                  pl.BlockSpec((1, 1, LANES), lambda b, h: (b, 0, 0)),
                  pl.BlockSpec((1, HD), lambda b, h: (0, 0)),
                  pl.BlockSpec((1, 1, 1, HD, HD), lambda b, h: (b, layer, h, 0, 0))],
        out_specs=[pl.BlockSpec((1, 1, HD), lambda b, h: (b, 0, h)),
                   pl.BlockSpec((1, 1, HD, HD), lambda b, h: (b, h, 0, 0))],
        out_shape=[jax.ShapeDtypeStruct((B, 1, H * HD), F32), jax.ShapeDtypeStruct((B, H, HD, HD), F32)],
        compiler_params=_cparams(2),
    )(z3, z3, z3, state_conv, state_conv, state_conv, conv_w, conv_w, conv_w, gp, onorm, state_delta)


def _fox_prompt_kernel(q_ref, k_ref, v_ref, gpt_ref, oa_ref, ga_ref, gb_ref, o_ref, kb_ref, va_ref, *, H, G):
    kvh = pl.program_id(1)
    qi = pl.program_id(2)
    tq = q_ref.shape[1]
    L = k_ref.shape[1]

    @pl.when(qi == 0)
    def _():
        kb_ref[...] = k_ref[0].astype(BF16)
        va_ref[:, :HD] = v_ref[0].astype(BF16)
        va_ref[:, HD:] = jnp.ones((L, HD), BF16)

    causal = lax.broadcasted_iota(jnp.int32, (tq, tq), 1) <= lax.broadcasted_iota(jnp.int32, (tq, tq), 0)
    scale = HD ** -0.5 * LOG2E
    dot = functools.partial(jnp.dot, preferred_element_type=F32)

    def attend(lo):
        for g in range(G):
            sl = slice(g * HD, (g + 1) * HD)
            qb = q_ref[0, :, sl].astype(BF16)
            ck = gpt_ref[0, pl.ds(3 * H + kvh * G + g, 1), :] * LOG2E
            s_own = jnp.where(causal, _dot_nt(qb, kb_ref[lo:lo + tq, :]) * scale - ck[:, lo:lo + tq], -jnp.inf)
            m = jnp.max(s_own, axis=1, keepdims=True)
            if lo:
                s_past = _dot_nt(qb, kb_ref[0:lo, :]) * scale - ck[:, 0:lo]
                m = jnp.maximum(m, jnp.max(s_past, axis=1, keepdims=True))
            pv = dot(jnp.exp2(s_own - m).astype(BF16), va_ref[lo:lo + tq, :])
            if lo:
                pv = pv + dot(jnp.exp2(s_past - m).astype(BF16), va_ref[0:lo, :])
            o_b = pv[:, :HD] / pv[:, HD:]
            o_ref[0, :, sl] = (jax.nn.sigmoid(ga_ref[0, :, sl]) * oa_ref[0, :, sl]
                               + jax.nn.sigmoid(gb_ref[0, :, sl]) * o_b).astype(o_ref.dtype)

    for n in range(L // tq):
        pl.when(qi == n)(functools.partial(attend, n * tq))


def _fox_prompt(z, gpt, o_a, zg, H, KV, q_col, k_col, v_col):
    B, L, _ = z.shape
    D = H * HD
    G = H // KV
    tq = _pick(L, 256, LANES)
    qw = G * HD
    blk = pl.BlockSpec((1, tq, qw), lambda b, kv, qi: (b, qi, kv))
    return pl.pallas_call(
        functools.partial(_fox_prompt_kernel, H=H, G=G),
        grid=(B, KV, L // tq),
        in_specs=[pl.BlockSpec((1, tq, qw), lambda b, kv, qi: (b, qi, q_col // qw + kv)),
                  pl.BlockSpec((1, L, HD), lambda b, kv, qi: (b, 0, k_col // HD + kv)),
                  pl.BlockSpec((1, L, HD), lambda b, kv, qi: (b, 0, v_col // HD + kv)),
                  pl.BlockSpec((1, LANES, L), lambda b, kv, qi: (b, 0, 0)),
                  blk, blk,
                  pl.BlockSpec((1, tq, qw), lambda b, kv, qi: (b, qi, D // qw + kv))],
        out_specs=blk,
        out_shape=jax.ShapeDtypeStruct((B, L, D), BF16),
        scratch_shapes=[pltpu.VMEM((L, HD), BF16), pltpu.VMEM((L, 2 * HD), BF16)],
        compiler_params=_cparams(3),
        name="fox_prompt",
    )(z, z, z, gpt, o_a, zg, zg)


def _fox_sample_kernel(pt_ref, q_ref, kn_ref, vn_ref, gp_ref, *refs, H, G, PP):
    k_refs, v_refs, lf_refs = refs[:PP], refs[PP:2 * PP], refs[2 * PP:3 * PP]
    o_ref, m_ref, l_ref, acc_ref, carry_ref, later_ref = refs[3 * PP:]
    p = pl.program_id(1)
    KV = H // G
    P = k_refs[0].shape[0]
    W = P * KV

    @pl.when(p == 0)
    def _():
        m_ref[...] = jnp.full(m_ref.shape, -jnp.inf, F32)
        l_ref[...] = jnp.zeros_like(l_ref)
        acc_ref[...] = jnp.zeros_like(acc_ref)
        r2 = lax.broadcasted_iota(jnp.int32, (H, LANES), 0)
        c2 = lax.broadcasted_iota(jnp.int32, (H, LANES), 1)
        lf_new = jnp.broadcast_to(gp_ref[0], (H, LANES))
        carry_ref[...] = jnp.sum(jnp.where(c2 == r2 + 2 * H, lf_new, 0.0), axis=1, keepdims=True)
        later_ref[...] = (lax.broadcasted_iota(jnp.int32, (P, W), 0)
                          > lax.broadcasted_iota(jnp.int32, (P, W), 1) // KV).astype(BF16)

    own = (lax.rem(lax.broadcasted_iota(jnp.int32, (H, W), 1), KV)
           == lax.broadcasted_iota(jnp.int32, (H, W), 0) // G)
    qb = q_ref[0].astype(BF16)
    later = later_ref[...]
    dot = functools.partial(jnp.dot, preferred_element_type=F32)
    lf = [lf_refs[i][...] for i in range(PP)]
    lf_hi = _each(lambda x: x.astype(BF16), lf)
    lf_r1 = _each(lambda x, h: x - h.astype(F32), lf, lf_hi)
    lf_mid = _each(lambda x: x.astype(BF16), lf_r1)
    lf_lo = _each(lambda x, m: (x - m.astype(F32)).astype(BF16), lf_r1, lf_mid)
    bias = _each(lambda a, b, c: dot(a, later) + (dot(b, later) + dot(c, later)), lf_hi, lf_mid, lf_lo)
    qk = [_dot_nt(qb, k_refs[i][...].reshape(W, HD).astype(BF16)) for i in range(PP)]
    carry = carry_ref[...]
    s = []
    for i in range(PP):
        s.append(jnp.where(own, qk[i] * (HD ** -0.5) + (bias[i] + carry), -jnp.inf))
        carry = carry + jnp.sum(lf[i], axis=1, keepdims=True)
    carry_ref[...] = carry
    m_old = m_ref[...]
    m_new = m_old
    for x in s:
        m_new = jnp.maximum(m_new, jnp.max(x, axis=1, keepdims=True))
    alpha = jnp.exp(m_old - m_new)
    pr = _each(lambda x: jnp.exp(x - m_new), s)
    l_new = alpha * l_ref[...]
    for x in pr:
        l_new = l_new + jnp.sum(x, axis=1, keepdims=True)
    l_ref[...] = l_new
    pv = [dot(pr[i].astype(BF16), v_refs[i][...].reshape(W, HD).astype(BF16)) for i in range(PP)]
    acc = alpha * acc_ref[...]
    for x in pv:
        acc = acc + x
    acc_ref[...] = acc
    m_ref[...] = m_new

    @pl.when(p == pl.num_programs(1) - 1)
    def _():
        s_new = jnp.sum(q_ref[0] * kn_ref[0], axis=1, keepdims=True) * (HD ** -0.5)
        m_old = m_ref[...]
        m_new = jnp.maximum(m_old, s_new)
        alpha = jnp.exp(m_old - m_new)
        p_new = jnp.exp(s_new - m_new)
        o_ref[0] = (alpha * acc_ref[...] + p_new * vn_ref[0]) / (alpha * l_ref[...] + p_new)


def _fox_sample(q, k_new, v_new, gp, cache_k, cache_v, cache_logf_t, page_table, layer, H, KV, PP=4):
    B, n_pages = page_table.shape
    P = cache_k.shape[2]
    G = H // KV
    while n_pages % PP:
        PP -= 1

    def page_spec(shape, i):
        return pl.BlockSpec((None, None) + shape, lambda b, p, pt: (pt[b, n_pages - 1 - (p * PP + i)], layer, 0, 0, 0)[:2 + len(shape)])

    row = pl.BlockSpec((1, H, HD), lambda b, p, pt: (b, 0, 0))
    grid_spec = pltpu.PrefetchScalarGridSpec(
        num_scalar_prefetch=1,
        grid=(B, n_pages // PP),
        in_specs=([row, row, row, pl.BlockSpec((1, 1, LANES), lambda b, p, pt: (b, 0, 0))]
                  + [page_spec((P, KV, HD), i) for i in range(PP)]
                  + [page_spec((P, KV, HD), i) for i in range(PP)]
                  + [page_spec((H, P), i) for i in range(PP)]),
        out_specs=row,
        scratch_shapes=[pltpu.VMEM((H, 1), F32), pltpu.VMEM((H, 1), F32), pltpu.VMEM((H, HD), F32),
                        pltpu.VMEM((H, 1), F32), pltpu.VMEM((P, P * KV), BF16)],
    )
    return pl.pallas_call(
        functools.partial(_fox_sample_kernel, H=H, G=G, PP=PP),
        grid_spec=grid_spec,
        out_shape=jax.ShapeDtypeStruct((B, H, HD), F32),
        compiler_params=_cparams(2),
        name="fox_sample",
    )(page_table, q, k_new, v_new, gp, *([cache_k] * PP), *([cache_v] * PP), *([cache_logf_t] * PP))


def _split6(m):
    return tuple(jnp.split(m, 6, axis=-1))


def kernel(x_prompt, x_sample, cache_k, cache_v, cache_logf, state_delta, state_conv, page_table, c_prompt, c_sample,
           norm_g, w_ada, b_ada, w_in, conv_w, a_log, dt_bias, b_f, onorm_g, w_o, w_gu, w_down):
    B, L, D = x_prompt.shape
    BS = x_sample.shape[0]
    depth = w_in.shape[0]
    H = D // HD
    KV = cache_k.shape[3]
    KVW = KV * HD
    F = w_down.shape[1]
    c_main = 3 * D + D + 2 * KVW
    c_gate = c_main + 3 * H
    q_col, k_col, v_col = 3 * D, 4 * D, 4 * D + KVW
    assert 4 * H <= LANES and w_in.shape[2] == c_gate + 2 * D

    n_c = B + BS
    pad = (-n_c) % SUBLANES
    c_all = jnp.concatenate([c_prompt, c_sample, jnp.zeros((pad, D), F32)], axis=0)
    mod = _adaln(c_all, w_ada, b_ada)

    w_in_t = jnp.swapaxes(w_in, 1, 2)
    zl = jnp.zeros((depth, H), F32)
    prm = jnp.stack([jnp.concatenate([zl, a_log, zl], 1), jnp.concatenate([zl, dt_bias, zl], 1),
                     jnp.concatenate([zl, zl, b_f], 1)], axis=1)
    prm = jnp.pad(prm, ((0, 0), (0, SUBLANES - 3), (0, LANES - 3 * H)))
    cache_logf_t = jnp.swapaxes(cache_logf, 2, 3)

    xp, xs = x_prompt, x_sample.reshape(1, BS, D)
    st_p = ([], [], [], [], [])
    st_s = ([], [], [], [], [])
    hp = hs = None
    for l in range(depth):
        mp = [m.reshape(B, 1, D) for m in _split6(mod[l, :B])]
        ms = [m.reshape(1, BS, D) for m in _split6(mod[l, B:n_c])]
        if l == 0:
            hp = _prenorm(xp, norm_g[l, 0], mp[1], mp[0])
            hs = _prenorm(xs, norm_g[l, 0], ms[1], ms[0])
        onorm = onorm_g[l].reshape(1, HD)

        h2 = hp.reshape(B * L, D)
        z = _matmul_nt(h2, w_in_t, l, row0=0, nrows=c_main).reshape(B, L, c_main)
        zs = _matmul_nt(h2, w_in_t, l, row0=c_main, nrows=LANES).reshape(B, L, LANES)
        zg = _matmul_nt(h2, w_in_t, l, row0=c_gate, nrows=2 * D).reshape(B, L, 2 * D)
        gp = _gates(zs, prm[l], H, True)
        gpt = jnp.swapaxes(gp, 1, 2)
        o_a, s_fin = _delta_prompt(z, gp, gpt, conv_w, onorm, l, H)
        m = _fox_prompt(z, gpt, o_a, zg, H, KV, q_col, k_col, v_col).reshape(B * L, D)
        y = _matmul(m, w_o, l).reshape(B, L, D)
        xp, hf = _resnorm(xp, y, norm_g[l, 1], mp[2], norm_g[l, 2], mp[4], mp[3])
        act = _matmul_swiglu(hf.reshape(B * L, D), w_gu, l)
        half = F // 2 if (F // 2) % LANES == 0 else F
        y = None
        for kb in range(F // half):
            y = _matmul(act, w_down, l, kblk=kb, kdim=half, add=y, tn=256)
        y = y.reshape(B, L, D)
        if l + 1 < depth:
            mnext = [mm.reshape(B, 1, D) for mm in _split6(mod[l + 1, :B])]
            xp, hp = _resnorm(xp, y, norm_g[l, 3], mp[5], norm_g[l + 1, 0], mnext[1], mnext[0])
        else:
            xp, _ = _resnorm(xp, y, norm_g[l, 3], mp[5])
        st_p[0].append(z[:, :, k_col:k_col + KVW].reshape(B, L, KV, HD))
        st_p[1].append(z[:, :, v_col:v_col + KVW].reshape(B, L, KV, HD))
        st_p[2].append(gp[:, :, 2 * H:3 * H])
        st_p[3].append(s_fin)
        st_p[4].append(z[:, L - (CONV_W - 1):, :3 * D])

        h2 = hs.reshape(BS, D)
        z = _matmul_nt(h2, w_in_t, l, row0=0, nrows=c_main)
        zs = _matmul_nt(h2, w_in_t, l, row0=c_main, nrows=LANES).reshape(BS, 1, LANES)
        zg = _matmul_nt(h2, w_in_t, l, row0=c_gate, nrows=2 * D).reshape(1, BS, 2 * D)
        gp = _gates(zs, prm[l], H, False)
        z3 = z.reshape(BS, 1, c_main)
        o_a, s_new = _delta_sample(z3, state_conv, state_delta, gp, conv_w, onorm, l, H)
        k_rep = jnp.repeat(z[:, k_col:k_col + KVW].reshape(BS, KV, HD), H // KV, axis=1)
        v_rep = jnp.repeat(z[:, v_col:v_col + KVW].reshape(BS, KV, HD), H // KV, axis=1)
        o_b = _fox_sample(z[:, q_col:q_col + D].reshape(BS, H, HD), k_rep, v_rep, gp, cache_k, cache_v,
                          cache_logf_t, page_table, l, H, KV)
        m = _merge(o_a.reshape(1, BS, D), o_b.reshape(1, BS, D), zg).reshape(BS, D)
        y = _matmul(m, w_o, l).reshape(1, BS, D)
        xs, hf = _resnorm(xs, y, norm_g[l, 1], ms[2], norm_g[l, 2], ms[4], ms[3])
        act = _matmul_swiglu(hf.reshape(BS, D), w_gu, l)
        y = None
        for kb in range(F // half):
            y = _matmul(act, w_down, l, kblk=kb, kdim=half, add=y, tn=256)
        y = y.reshape(1, BS, D)
        if l + 1 < depth:
            mnext = [mm.reshape(1, BS, D) for mm in _split6(mod[l + 1, B:n_c])]
            xs, hs = _resnorm(xs, y, norm_g[l, 3], ms[5], norm_g[l + 1, 0], mnext[1], mnext[0])
        else:
            xs, _ = _resnorm(xs, y, norm_g[l, 3], ms[5])
        st_s[0].append(z[:, k_col:k_col + KVW].reshape(BS, 1, KV, HD))
        st_s[1].append(z[:, v_col:v_col + KVW].reshape(BS, 1, KV, HD))
        st_s[2].append(gp[:, :, 2 * H:3 * H])
        st_s[3].append(s_new)
        st_s[4].append(jnp.concatenate([state_conv[:, l, 1:], z3[:, :, :3 * D]], axis=1))

    k_p, v_p, lf_p, d_p, cv_p = (jnp.stack(lst, axis=1) for lst in st_p)
    k_s, v_s, lf_s, d_s, cv_s = (jnp.stack(lst, axis=1) for lst in st_s)
    return (xp, xs.reshape(BS, 1, D), k_p, v_p, lf_p, d_p, cv_p, k_s, v_s, lf_s, d_s, cv_s)
```

```python
import functools

import jax
import jax.numpy as jnp
from jax import lax
from jax.experimental import pallas as pl
from jax.experimental.pallas import tpu as pltpu

HD = 128
CONV_W = 4
CHUNK = 64
NORM_EPS = 1e-6
L2_EPS = 1e-6
LOG2E = 1.4426950408889634
LANES = 128
SUBLANES = 8
VMEM_LIMIT = 56 * 1024 * 1024
HI = lax.Precision.HIGHEST
F32 = jnp.float32
BF16 = jnp.bfloat16


def _cparams(n_axes):
    return pltpu.CompilerParams(dimension_semantics=("arbitrary",) * n_axes, vmem_limit_bytes=VMEM_LIMIT)


def _pick(n, pref, align):
    if n <= pref:
        return n
    t = (pref // align) * align
    while t > align and n % t:
        t -= align
    assert n % t == 0, (n, pref, align)
    return t


def _silu(x):
    return x * jax.nn.sigmoid(x)


def _softplus(x):
    return jnp.maximum(x, 0.0) + jnp.log1p(jnp.exp(-jnp.abs(x)))


def _rms(x, g):
    return x * lax.rsqrt(jnp.mean(x * x, axis=-1, keepdims=True) + NORM_EPS) * g


def _dot_nt(a, b, **kw):
    return lax.dot_general(a, b, (((1,), (1,)), ((), ())), preferred_element_type=F32, **kw)


def _dot_tn(a, b, **kw):
    return lax.dot_general(a, b, (((0,), (0,)), ((), ())), preferred_element_type=F32, **kw)


def _linear_kernel(x_ref, xs_ref, *refs, n_w, nt, cast, swiglu):
    w_refs, (o_ref, os_ref), wb_refs = refs[:n_w], refs[n_w:n_w + 2], refs[n_w + 2:]
    first = pl.program_id(1) == 0
    if cast:
        @pl.when(first)
        def _():
            for w_ref, wb_ref in zip(w_refs, wb_refs):
                wb_ref[...] = w_ref[0].astype(BF16)

    def apply(x):
        ws = [wb_ref[...] for wb_ref in wb_refs] if cast else [w_ref[0] for w_ref in w_refs]
        ys = [_dot_nt(x, w) if nt else jnp.dot(x, w, preferred_element_type=F32) for w in ws]
        return _silu(ys[0]) * ys[1] if swiglu else ys[0]

    o_ref[...] = apply(x_ref[...]).astype(o_ref.dtype)

    @pl.when(first)
    def _():
        os_ref[...] = apply(xs_ref[...]).astype(os_ref.dtype)


def _linear(x, xs, w, layer, *, nt=False, start=0, size=None, swiglu=False, out_dtype=F32, tm=1024, tn=512):
    M, K = x.shape
    Ms = xs.shape[0]
    n_all = w.shape[1] if nt else w.shape[2]
    if swiglu:
        assert not nt and start == 0 and size is None
        size = n_all // 2
    size = n_all - start if size is None else size
    tm = _pick(M, tm, SUBLANES)
    tn = _pick(size, tn, LANES)
    nb = size // tn
    cast = w.dtype != BF16

    def w_spec(extra):
        if not nt:
            assert start % tn == 0
            return pl.BlockSpec((1, K, tn), lambda j, i: (layer, 0, start // tn + extra + j))
        if start % tn == 0:
            return pl.BlockSpec((1, tn, K), lambda j, i: (layer, start // tn + j, 0))
        assert start % SUBLANES == 0
        return pl.BlockSpec((pl.Element(1), pl.Element(tn), pl.Element(K)),
                            lambda j, i: (layer, pl.multiple_of(start + j * tn, SUBLANES), 0))

    n_w = 2 if swiglu else 1
    w_specs = [w_spec(0), w_spec(nb)][:n_w]
    w_block = (tn, K) if nt else (K, tn)
    return pl.pallas_call(
        functools.partial(_linear_kernel, n_w=n_w, nt=nt, cast=cast, swiglu=swiglu),
        grid=(nb, M // tm),
        in_specs=[pl.BlockSpec((tm, K), lambda j, i: (i, 0)), pl.BlockSpec((Ms, K), lambda j, i: (0, 0))] + w_specs,
        out_specs=[pl.BlockSpec((tm, tn), lambda j, i: (i, j)), pl.BlockSpec((Ms, tn), lambda j, i: (0, j))],
        out_shape=[jax.ShapeDtypeStruct((M, size), out_dtype), jax.ShapeDtypeStruct((Ms, size), out_dtype)],
        scratch_shapes=[pltpu.VMEM(w_block, BF16)] * (n_w if cast else 0),
        compiler_params=_cparams(2),
        name="linear",
    )(x, xs, *([w] * n_w))


def _adaln_kernel(c_ref, w_ref, b_ref, o_ref):
    a = _silu(c_ref[...]).astype(BF16)
    o_ref[0] = jnp.dot(a, w_ref[0].astype(BF16), preferred_element_type=F32) + b_ref[0]


def _adaln(c_all, w_ada, b_ada):
    depth, D, N = w_ada.shape
    R = c_all.shape[0]
    tn = _pick(N, 512, LANES)
    return pl.pallas_call(
        _adaln_kernel,
        grid=(depth, N // tn),
        in_specs=[pl.BlockSpec((R, D), lambda l, j: (0, 0)),
                  pl.BlockSpec((1, D, tn), lambda l, j: (l, 0, j)),
                  pl.BlockSpec((1, 1, tn), lambda l, j: (l, 0, j))],
        out_specs=pl.BlockSpec((1, R, tn), lambda l, j: (l, 0, j)),
        out_shape=jax.ShapeDtypeStruct((depth, R, N), F32),
        compiler_params=_cparams(2),
    )(c_all, w_ada, b_ada.reshape(depth, 1, N))


def _prenorm_kernel(x_ref, g_ref, sc_ref, sh_ref, h_ref):
    h_ref[0] = (_rms(x_ref[0], g_ref[...]) * (1.0 + sc_ref[0]) + sh_ref[0]).astype(h_ref.dtype)


def _resnorm_kernel(x_ref, y_ref, gpost_ref, gt_ref, gpre_ref, sc_ref, sh_ref, xo_ref, h_ref):
    xn = x_ref[0] + gt_ref[0] * _rms(y_ref[0], gpost_ref[...])
    xo_ref[0] = xn
    h_ref[0] = (_rms(xn, gpre_ref[...]) * (1.0 + sc_ref[0]) + sh_ref[0]).astype(h_ref.dtype)


def _res_kernel(x_ref, y_ref, gpost_ref, gt_ref, xo_ref):
    xo_ref[0] = x_ref[0] + gt_ref[0] * _rms(y_ref[0], gpost_ref[...])


def _row_specs(B, T, D, tl, n_mod):
    act = pl.BlockSpec((1, tl, D), lambda b, t: (b, t, 0))
    gain = pl.BlockSpec((1, D), lambda b, t: (0, 0))
    return act, gain


def _mod_spec(m, tl):
    S = m.shape[1]
    if S == 1:
        return pl.BlockSpec((1, 1, m.shape[2]), lambda b, t: (b, 0, 0))
    return pl.BlockSpec((1, tl, m.shape[2]), lambda b, t: (b, t, 0))


def _prenorm(x, g, sc, sh):
    B, T, D = x.shape
    tl = _pick(T, 256, SUBLANES)
    act, gain = _row_specs(B, T, D, tl, 2)
    return pl.pallas_call(
        _prenorm_kernel,
        grid=(B, T // tl),
        in_specs=[act, gain, _mod_spec(sc, tl), _mod_spec(sh, tl)],
        out_specs=act,
        out_shape=jax.ShapeDtypeStruct((B, T, D), BF16),
        compiler_params=_cparams(2),
    )(x, g.reshape(1, D), sc, sh)


def _resnorm(x, y, gpost, gt, gpre=None, sc=None, sh=None):
    B, T, D = x.shape
    tl = _pick(T, 256, SUBLANES)
    act, gain = _row_specs(B, T, D, tl, 3)
    if gpre is None:
        return pl.pallas_call(
            _res_kernel,
            grid=(B, T // tl),
            in_specs=[act, act, gain, _mod_spec(gt, tl)],
            out_specs=act,
            out_shape=jax.ShapeDtypeStruct((B, T, D), F32),
            compiler_params=_cparams(2),
        )(x, y, gpost.reshape(1, D), gt), None
    return pl.pallas_call(
        _resnorm_kernel,
        grid=(B, T // tl),
        in_specs=[act, act, gain, _mod_spec(gt, tl), gain, _mod_spec(sc, tl), _mod_spec(sh, tl)],
        out_specs=[act, act],
        out_shape=[jax.ShapeDtypeStruct((B, T, D), F32), jax.ShapeDtypeStruct((B, T, D), BF16)],
        compiler_params=_cparams(2),
    )(x, y, gpost.reshape(1, D), gt, gpre.reshape(1, D), sc, sh)


def _merge_kernel(oa_ref, ob_ref, ga_ref, gb_ref, m_ref):
    m = jax.nn.sigmoid(ga_ref[0]) * oa_ref[0] + jax.nn.sigmoid(gb_ref[0]) * ob_ref[0]
    m_ref[0] = m.astype(m_ref.dtype)


def _merge(o_a, o_b, zg):
    B, T, W = o_a.shape
    tl = _pick(T, 256, SUBLANES)
    tc = _pick(W, 1024, LANES)
    nc = W // tc
    spec = pl.BlockSpec((1, tl, tc), lambda b, t, c: (b, t, c))
    return pl.pallas_call(
        _merge_kernel,
        grid=(B, T // tl, nc),
        in_specs=[spec, spec, spec, pl.BlockSpec((1, tl, tc), lambda b, t, c: (b, t, nc + c))],
        out_specs=spec,
        out_shape=jax.ShapeDtypeStruct((B, T, W), BF16),
        compiler_params=_cparams(3),
    )(o_a, o_b, zg, zg)


def _gates_kernel(z_ref, prm_ref, o_ref, carry_ref, *, H, cumulative):
    z = z_ref[0]
    tl = z.shape[0]
    a_log, dt_bias, b_f = prm_ref[0:1, :], prm_ref[1:2, :], prm_ref[2:3, :]
    lane = lax.broadcasted_iota(jnp.int32, z.shape, 1)
    beta = jax.nn.sigmoid(z)
    g = -jnp.exp(a_log) * _softplus(z + dt_bias)
    lf = -_softplus(-(z + b_f))
    if cumulative:
        row = lax.broadcasted_iota(jnp.int32, (tl, tl), 0)
        col = lax.broadcasted_iota(jnp.int32, (tl, tl), 1)
        tri = row >= col
        in_chunk = tri & ((row // CHUNK) == (col // CHUNK))
        g = jnp.dot(in_chunk.astype(F32), g, precision=HI, preferred_element_type=F32)

        @pl.when(pl.program_id(1) == 0)
        def _():
            carry_ref[...] = jnp.zeros_like(carry_ref)

        c = jnp.dot(tri.astype(F32), lf, precision=HI, preferred_element_type=F32) + carry_ref[...]
        carry_ref[...] = c[tl - 1:tl, :]
    else:
        c = lf
    c = pltpu.roll(c, H, axis=1)
    o_ref[0] = jnp.where(lane < H, beta, jnp.where(lane < 2 * H, g, jnp.where(lane < 3 * H, lf, c)))


def _gates(zs, prm, H, cumulative):
    B, T, _ = zs.shape
    tl = _pick(T, 256, CHUNK) if cumulative else T
    spec = pl.BlockSpec((1, tl, LANES), lambda b, t: (b, t, 0))
    return pl.pallas_call(
        functools.partial(_gates_kernel, H=H, cumulative=cumulative),
        grid=(B, T // tl),
        in_specs=[spec, pl.BlockSpec((SUBLANES, LANES), lambda b, t: (0, 0))],
        out_specs=spec,
        out_shape=jax.ShapeDtypeStruct((B, T, LANES), F32),
        scratch_shapes=[pltpu.VMEM((1, LANES), F32)],
        compiler_params=_cparams(2),
    )(zs, prm)


def _lane_col(x, idx):
    lane = lax.broadcasted_iota(jnp.int32, x.shape, 1)
    return jnp.sum(jnp.where(lane == idx, x, 0.0), axis=1, keepdims=True)


def _bf16_dot(a, b):
    return jnp.dot(a.astype(BF16), b.astype(BF16), preferred_element_type=F32)


def _split_bf16(x):
    hi = x.astype(BF16)
    return hi, (x - hi.astype(F32)).astype(BF16)


def _dot_3pass(a, b):
    a_hi, a_lo = _split_bf16(a)
    b_hi, b_lo = _split_bf16(b)
    dot = functools.partial(jnp.dot, preferred_element_type=F32)
    return dot(a_hi, b_hi) + (dot(a_hi, b_lo) + dot(a_lo, b_hi))


def _each(f, *lists):
    return [f(*args) for args in zip(*lists)]


def _unit_lower_inverse(a_list, dot):
    n = a_list[0].shape[0]
    row = lax.broadcasted_iota(jnp.int32, (n, n), 0)
    col = lax.broadcasted_iota(jnp.int32, (n, n), 1)
    eye = (row == col).astype(F32)
    base = SUBLANES
    diag = (row // base) == (col // base)
    ad = _each(lambda a: jnp.where(diag, a, 0.0), a_list)
    a2 = _each(lambda x: dot(x, x), ad)
    a4 = _each(lambda x: dot(x, x), a2)
    t = _each(lambda x, y: dot(eye - x, eye + y), ad, a2)
    t = _each(lambda x, y: dot(x, eye + y), t, a4)
    s = base
    while s < n:
        off = ((row // (2 * s)) == (col // (2 * s))) & ((row // s) != (col // s))
        tm = _each(lambda x, a: dot(x, jnp.where(off, a, 0.0)), t, a_list)
        t = _each(lambda x, y: x - dot(y, x), t, tm)
        s *= 2
    return t


def _unit_lower_solve(a_list, rhs_list):
    t0 = _each(lambda t: t.astype(BF16), _unit_lower_inverse(a_list, _bf16_dot))
    x0 = _each(lambda t, b: jnp.dot(t, b.astype(BF16), preferred_element_type=F32), t0, rhs_list)
    r = _each(lambda a, b, x: b - x - _dot_3pass(a, x), a_list, rhs_list, x0)
    return _each(lambda t, x, rr: x + jnp.dot(t, rr.astype(BF16), preferred_element_type=F32), t0, x0, r)


def _conv_rows(xb, i, w_ref, n):
    ext = xb[i, 0:SUBLANES + n, :]
    acc = ext[SUBLANES:] * w_ref[0, CONV_W - 1:CONV_W, :]
    for s in range(1, CONV_W):
        back = pltpu.roll(ext, s, axis=0)[SUBLANES:]
        acc = acc + back * w_ref[0, CONV_W - 1 - s:CONV_W - s, :]
    return acc


def _l2n(t):
    return t * lax.rsqrt(jnp.sum(t * t, axis=-1, keepdims=True) + L2_EPS)


def _delta_prompt_kernel(zq_ref, zk_ref, zv_ref, wq_ref, wk_ref, wv_ref, gp_ref, gpt_ref, on_ref,
                         o_ref, sfin_ref, s_ref, xb, *, H, HB):
    hg = pl.program_id(1)
    t = pl.program_id(2)
    Lb = zq_ref.shape[1]

    @pl.when(t == 0)
    def _():
        s_ref[...] = jnp.zeros_like(s_ref)
        xb[:, 0:SUBLANES, :] = jnp.zeros((3, SUBLANES, HB * HD), F32)

    @pl.when(t > 0)
    def _():
        xb[:, 0:SUBLANES, :] = xb[:, Lb:Lb + SUBLANES, :]

    xb[0, SUBLANES:SUBLANES + Lb, :] = zq_ref[0]
    xb[1, SUBLANES:SUBLANES + Lb, :] = zk_ref[0]
    xb[2, SUBLANES:SUBLANES + Lb, :] = zv_ref[0]
    q_all = _silu(_conv_rows(xb, 0, wq_ref, Lb))
    k_all = _silu(_conv_rows(xb, 1, wk_ref, Lb))
    v_all = _silu(_conv_rows(xb, 2, wv_ref, Lb))

    gp = gp_ref[0]
    row = lax.broadcasted_iota(jnp.int32, (CHUNK, CHUNK), 0)
    col = lax.broadcasted_iota(jnp.int32, (CHUNK, CHUNK), 1)
    nc = Lb // CHUNK
    decay, kq_lhs, k_rhs, rhs_l, qg_l, kd_l, gl_l = [], [], [], [], [], [], []
    for j in range(HB):
        head = hg * HB + j
        hs = slice(j * HD, (j + 1) * HD)
        q = _l2n(q_all[:, hs]) * (HD ** -0.5)
        k = _l2n(k_all[:, hs])
        v = v_all[:, hs]
        beta = _lane_col(gp, head)
        gcum = _lane_col(gp, head + H)
        grow = gpt_ref[0, pl.ds(head + H, 1), :]
        eg = jnp.exp(gcum)
        kb = k * beta
        rhs = jnp.concatenate([v * beta, kb * eg], axis=1)
        qg = q * eg
        for c in range(nc):
            sl = slice(c * CHUNK, (c + 1) * CHUNK)
            gc = gcum[sl]
            g_last = gc[CHUNK - 1:CHUNK, :]
            decay.append(jnp.exp(jnp.where(row >= col, gc - grow[:, sl], -jnp.inf)))
            kq_lhs.append(jnp.concatenate([kb[sl], q[sl]], axis=0).astype(BF16))
            k_rhs.append(k[sl].astype(BF16))
            rhs_l.append(rhs[sl])
            qg_l.append(qg[sl])
            kd_l.append((k[sl] * jnp.exp(g_last - gc)).astype(BF16))
            gl_l.append(jnp.exp(g_last))
    kq = _each(_dot_nt, kq_lhs, k_rhs)
    a_mat = _each(lambda x, d: jnp.where(row > col, x[:CHUNK] * d, 0.0), kq, decay)
    a_qk = _each(lambda x, d: (x[CHUNK:] * d).astype(BF16), kq, decay)
    sol = _unit_lower_solve(a_mat, rhs_l)
    wq = _each(lambda x, g: jnp.concatenate([x[:, HD:], g], axis=0).astype(BF16), sol, qg_l)

    s = [s_ref[j] for j in range(HB)]
    outs = [[] for _ in range(HB)]
    for c in range(nc):
        idx = [j * nc + c for j in range(HB)]
        ws_qs = [jnp.dot(wq[i], s[j].astype(BF16), preferred_element_type=F32) for j, i in enumerate(idx)]
        db = [(sol[i][:, :HD] - ws_qs[j][:CHUNK]).astype(BF16) for j, i in enumerate(idx)]
        for j, i in enumerate(idx):
            outs[j].append(ws_qs[j][CHUNK:] + jnp.dot(a_qk[i], db[j], preferred_element_type=F32))
        s = [gl_l[i] * s[j] + _dot_tn(kd_l[i], db[j]) for j, i in enumerate(idx)]
    for j in range(HB):
        s_ref[j] = s[j]
        o_ref[0, :, j * HD:(j + 1) * HD] = _rms(jnp.concatenate(outs[j], axis=0), on_ref[...])

    @pl.when(t == pl.num_programs(2) - 1)
    def _():
        sfin_ref[0] = s_ref[...]


def _delta_prompt(z, gp, gpt, conv_w, onorm, layer, H, HB=4):
    B, L, _ = z.shape
    Lb = _pick(L, 256, CHUNK)
    HB = HB if H % HB == 0 else 1
    ng = H // HB

    def zspec(off):
        return pl.BlockSpec((1, Lb, HB * HD), lambda b, h, t: (b, t, off + h))

    def wspec(off):
        return pl.BlockSpec((1, CONV_W, HB * HD), lambda b, h, t: (layer, 0, off + h))

    return pl.pallas_call(
        functools.partial(_delta_prompt_kernel, H=H, HB=HB),
        grid=(B, ng, L // Lb),
        in_specs=[zspec(0), zspec(ng), zspec(2 * ng), wspec(0), wspec(ng), wspec(2 * ng),
                  pl.BlockSpec((1, Lb, LANES), lambda b, h, t: (b, t, 0)),
                  pl.BlockSpec((1, LANES, Lb), lambda b, h, t: (b, 0, t)),
                  pl.BlockSpec((1, HD), lambda b, h, t: (0, 0))],
        out_specs=[pl.BlockSpec((1, Lb, HB * HD), lambda b, h, t: (b, t, h)),
                   pl.BlockSpec((1, HB, HD, HD), lambda b, h, t: (b, h, 0, 0))],
        out_shape=[jax.ShapeDtypeStruct((B, L, H * HD), F32), jax.ShapeDtypeStruct((B, H, HD, HD), F32)],
        scratch_shapes=[pltpu.VMEM((HB, HD, HD), F32), pltpu.VMEM((3, Lb + SUBLANES, HB * HD), F32)],
        compiler_params=_cparams(3),
        name="delta_prompt",
    )(z, z, z, conv_w, conv_w, conv_w, gp, gpt, onorm)


def _delta_sample_kernel(zq_ref, zk_ref, zv_ref, bq_ref, bk_ref, bv_ref, wq_ref, wk_ref, wv_ref, gp_ref, on_ref,
                         s0_ref, o_ref, s_ref, *, H, HB):
    hg = pl.program_id(1)

    def conv(z_ref, b_ref, w_ref):
        acc = z_ref[0] * w_ref[0, CONV_W - 1:CONV_W, :]
        for j in range(CONV_W - 1):
            acc = acc + b_ref[0, 0, j:j + 1, :] * w_ref[0, j:j + 1, :]
        return _silu(acc)

    q_all = conv(zq_ref, bq_ref, wq_ref)
    k_all = conv(zk_ref, bk_ref, wk_ref)
    v_all = conv(zv_ref, bv_ref, wv_ref)
    gp = gp_ref[0]
    hs = [slice(j * HD, (j + 1) * HD) for j in range(HB)]
    q = [_l2n(q_all[:, sl]) * (HD ** -0.5) for sl in hs]
    k = [_l2n(k_all[:, sl]) for sl in hs]
    v = [v_all[:, sl] for sl in hs]
    beta = [_lane_col(gp, hg * HB + j) for j in range(HB)]
    g = [_lane_col(gp, hg * HB + j + H) for j in range(HB)]
    dot = functools.partial(jnp.dot, precision=HI, preferred_element_type=F32)
    row = lax.broadcasted_iota(jnp.int32, (HD, HD), 0)
    s = [jnp.exp(g[j]) * s0_ref[0, 0, j] for j in range(HB)]
    ks = _each(lambda kk, ss: dot(jnp.broadcast_to(kk, (SUBLANES, HD)), ss)[0:1], k, s)
    delta = _each(lambda b, vv, x: b * (vv - x), beta, v, ks)
    outer = _each(lambda kk, dd: _dot_tn(jnp.where(row == 0, jnp.broadcast_to(kk, (HD, HD)), 0.0),
                                         jnp.broadcast_to(dd, (HD, HD)), precision=HI), k, delta)
    s = _each(lambda a, b: a + b, s, outer)
    o = _each(lambda qq, ss: dot(jnp.broadcast_to(qq, (SUBLANES, HD)), ss)[0:1], q, s)
    for j in range(HB):
        s_ref[0, j] = s[j]
        o_ref[0, :, hs[j]] = _rms(o[j], on_ref[...])


def _delta_sample(z3, state_conv, state_delta, gp, conv_w, onorm, layer, H, HB=8):
    B = z3.shape[0]
    HB = HB if H % HB == 0 else 1
    ng = H // HB

    def zspec(off):
        return pl.BlockSpec((1, 1, HB * HD), lambda b, h: (b, 0, off + h))

    def bspec(off):
        return pl.BlockSpec((1, 1, CONV_W - 1, HB * HD), lambda b, h: (b, layer, 0, off + h))

    def wspec(off):
        return pl.BlockSpec((1, CONV_W, HB * HD), lambda b, h: (layer, 0, off + h))

    return pl.pallas_call(
        functools.partial(_delta_sample_kernel, H=H, HB=HB),
        grid=(B, ng),
        in_specs=[zspec(0), zspec(ng), zspec(2 * ng), bspec(0), bspec(ng), bspec(2 * ng),
                  wspec(0), wspec(ng), wspec(2 * ng),
                  pl.BlockSpec((1, 1, LANES), lambda b, h: (b, 0, 0)),
                  pl.BlockSpec((1, HD), lambda b, h: (0, 0)),
                  pl.BlockSpec((1, 1, HB, HD, HD), lambda b, h: (b, layer, h, 0, 0))],
        out_specs=[pl.BlockSpec((1, 1, HB * HD), lambda b, h: (b, 0, h)),
                   pl.BlockSpec((1, HB, HD, HD), lambda b, h: (b, h, 0, 0))],
        out_shape=[jax.ShapeDtypeStruct((B, 1, H * HD), F32), jax.ShapeDtypeStruct((B, H, HD, HD), F32)],
        compiler_params=_cparams(2),
        name="delta_sample",
    )(z3, z3, z3, state_conv, state_conv, state_conv, conv_w, conv_w, conv_w, gp, onorm, state_delta)


def _fox_prompt_kernel(q_ref, k_ref, v_ref, gpt_ref, oa_ref, ga_ref, gb_ref, o_ref, kb_ref, va_ref, *, H, G):
    kvh = pl.program_id(1)
    qi = pl.program_id(2)
    tq = q_ref.shape[1]
    L = k_ref.shape[1]

    @pl.when(qi == 0)
    def _():
        kb_ref[...] = k_ref[0].astype(BF16)
        va_ref[:, :HD] = v_ref[0].astype(BF16)
        va_ref[:, HD:] = jnp.ones((L, HD), BF16)

    causal = lax.broadcasted_iota(jnp.int32, (tq, tq), 1) <= lax.broadcasted_iota(jnp.int32, (tq, tq), 0)
    scale = HD ** -0.5 * LOG2E
    dot = functools.partial(jnp.dot, preferred_element_type=F32)

    def attend(lo):
        for g in range(G):
            sl = slice(g * HD, (g + 1) * HD)
            qb = q_ref[0, :, sl].astype(BF16)
            ck = gpt_ref[0, pl.ds(3 * H + kvh * G + g, 1), :] * LOG2E
            s_own = jnp.where(causal, _dot_nt(qb, kb_ref[lo:lo + tq, :]) * scale - ck[:, lo:lo + tq], -jnp.inf)
            m = jnp.max(s_own, axis=1, keepdims=True)
            if lo:
                s_past = _dot_nt(qb, kb_ref[0:lo, :]) * scale - ck[:, 0:lo]
                m = jnp.maximum(m, jnp.max(s_past, axis=1, keepdims=True))
            pv = dot(jnp.exp2(s_own - m).astype(BF16), va_ref[lo:lo + tq, :])
            if lo:
                pv = pv + dot(jnp.exp2(s_past - m).astype(BF16), va_ref[0:lo, :])
            o_b = pv[:, :HD] / pv[:, HD:]
            o_ref[0, :, sl] = (jax.nn.sigmoid(ga_ref[0, :, sl]) * oa_ref[0, :, sl]
                               + jax.nn.sigmoid(gb_ref[0, :, sl]) * o_b).astype(o_ref.dtype)

    for n in range(L // tq):
        pl.when(qi == n)(functools.partial(attend, n * tq))


def _fox_prompt(z, gpt, o_a, zg, H, KV, q_col, k_col, v_col):
    B, L, _ = z.shape
    D = H * HD
    G = H // KV
    tq = _pick(L, 256, LANES)
    qw = G * HD
    blk = pl.BlockSpec((1, tq, qw), lambda b, kv, qi: (b, qi, kv))
    return pl.pallas_call(
        functools.partial(_fox_prompt_kernel, H=H, G=G),
        grid=(B, KV, L // tq),
        in_specs=[pl.BlockSpec((1, tq, qw), lambda b, kv, qi: (b, qi, q_col // qw + kv)),
                  pl.BlockSpec((1, L, HD), lambda b, kv, qi: (b, 0, k_col // HD + kv)),
                  pl.BlockSpec((1, L, HD), lambda b, kv, qi: (b, 0, v_col // HD + kv)),
                  pl.BlockSpec((1, LANES, L), lambda b, kv, qi: (b, 0, 0)),
                  blk, blk,
                  pl.BlockSpec((1, tq, qw), lambda b, kv, qi: (b, qi, D // qw + kv))],
        out_specs=blk,
        out_shape=jax.ShapeDtypeStruct((B, L, D), BF16),
        scratch_shapes=[pltpu.VMEM((L, HD), BF16), pltpu.VMEM((L, 2 * HD), BF16)],
        compiler_params=_cparams(3),
        name="fox_prompt",
    )(z, z, z, gpt, o_a, zg, zg)


def _fox_sample_kernel(pt_ref, q_ref, kn_ref, vn_ref, gp_ref, *refs, H, G, PP):
    k_refs, v_refs, lf_refs = refs[:PP], refs[PP:2 * PP], refs[2 * PP:3 * PP]
    o_ref, m_ref, l_ref, acc_ref, carry_ref, later_ref = refs[3 * PP:]
    p = pl.program_id(1)
    KV = H // G
    P = k_refs[0].shape[0]
    W = P * KV

    @pl.when(p == 0)
    def _():
        m_ref[...] = jnp.full(m_ref.shape, -jnp.inf, F32)
        l_ref[...] = jnp.zeros_like(l_ref)
        acc_ref[...] = jnp.zeros_like(acc_ref)
        r2 = lax.broadcasted_iota(jnp.int32, (H, LANES), 0)
        c2 = lax.broadcasted_iota(jnp.int32, (H, LANES), 1)
        lf_new = jnp.broadcast_to(gp_ref[0], (H, LANES))
        carry_ref[...] = jnp.sum(jnp.where(c2 == r2 + 2 * H, lf_new, 0.0), axis=1, keepdims=True)
        later_ref[...] = (lax.broadcasted_iota(jnp.int32, (P, W), 0)
                          > lax.broadcasted_iota(jnp.int32, (P, W), 1) // KV).astype(BF16)

    own = (lax.rem(lax.broadcasted_iota(jnp.int32, (H, W), 1), KV)
           == lax.broadcasted_iota(jnp.int32, (H, W), 0) // G)
    qb = q_ref[0].astype(BF16)
    later = later_ref[...]
    dot = functools.partial(jnp.dot, preferred_element_type=F32)
    lf = [lf_refs[i][...] for i in range(PP)]
    lf_hi = _each(lambda x: x.astype(BF16), lf)
    lf_r1 = _each(lambda x, h: x - h.astype(F32), lf, lf_hi)
    lf_mid = _each(lambda x: x.astype(BF16), lf_r1)
    lf_lo = _each(lambda x, m: (x - m.astype(F32)).astype(BF16), lf_r1, lf_mid)
    bias = _each(lambda a, b, c: dot(a, later) + (dot(b, later) + dot(c, later)), lf_hi, lf_mid, lf_lo)
    qk = [_dot_nt(qb, k_refs[i][...].reshape(W, HD).astype(BF16)) for i in range(PP)]
    carry = carry_ref[...]
    s = []
    for i in range(PP):
        s.append(jnp.where(own, qk[i] * (HD ** -0.5) + (bias[i] + carry), -jnp.inf))
        carry = carry + jnp.sum(lf[i], axis=1, keepdims=True)
    carry_ref[...] = carry
    m_old = m_ref[...]
    m_new = m_old
    for x in s:
        m_new = jnp.maximum(m_new, jnp.max(x, axis=1, keepdims=True))
    alpha = jnp.exp(m_old - m_new)
    pr = _each(lambda x: jnp.exp(x - m_new), s)
    l_new = alpha * l_ref[...]
    for x in pr:
        l_new = l_new + jnp.sum(x, axis=1, keepdims=True)
    l_ref[...] = l_new
    pv = [dot(pr[i].astype(BF16), v_refs[i][...].reshape(W, HD).astype(BF16)) for i in range(PP)]
    acc = alpha * acc_ref[...]
    for x in pv:
        acc = acc + x
    acc_ref[...] = acc
    m_ref[...] = m_new

    @pl.when(p == pl.num_programs(1) - 1)
    def _():
        s_new = jnp.sum(q_ref[0] * kn_ref[0], axis=1, keepdims=True) * (HD ** -0.5)
        m_old = m_ref[...]
        m_new = jnp.maximum(m_old, s_new)
        alpha = jnp.exp(m_old - m_new)
        p_new = jnp.exp(s_new - m_new)
        o_ref[0] = (alpha * acc_ref[...] + p_new * vn_ref[0]) / (alpha * l_ref[...] + p_new)


def _fox_sample(q, k_new, v_new, gp, cache_k, cache_v, cache_logf_t, page_table, layer, H, KV, PP=4):
    B, n_pages = page_table.shape
    P = cache_k.shape[2]
    G = H // KV
    while n_pages % PP:
        PP -= 1

    def page_spec(shape, i):
        return pl.BlockSpec((None, None) + shape, lambda b, p, pt: (pt[b, n_pages - 1 - (p * PP + i)], layer, 0, 0, 0)[:2 + len(shape)])

    row = pl.BlockSpec((1, H, HD), lambda b, p, pt: (b, 0, 0))
    grid_spec = pltpu.PrefetchScalarGridSpec(
        num_scalar_prefetch=1,
        grid=(B, n_pages // PP),
        in_specs=([row, row, row, pl.BlockSpec((1, 1, LANES), lambda b, p, pt: (b, 0, 0))]
                  + [page_spec((P, KV, HD), i) for i in range(PP)]
                  + [page_spec((P, KV, HD), i) for i in range(PP)]
                  + [page_spec((H, P), i) for i in range(PP)]),
        out_specs=row,
        scratch_shapes=[pltpu.VMEM((H, 1), F32), pltpu.VMEM((H, 1), F32), pltpu.VMEM((H, HD), F32),
                        pltpu.VMEM((H, 1), F32), pltpu.VMEM((P, P * KV), BF16)],
    )
    return pl.pallas_call(
        functools.partial(_fox_sample_kernel, H=H, G=G, PP=PP),
        grid_spec=grid_spec,
        out_shape=jax.ShapeDtypeStruct((B, H, HD), F32),
        compiler_params=_cparams(2),
        name="fox_sample",
    )(page_table, q, k_new, v_new, gp, *([cache_k] * PP), *([cache_v] * PP), *([cache_logf_t] * PP))


def _split6(m):
    return tuple(jnp.split(m, 6, axis=-1))


def kernel(x_prompt, x_sample, cache_k, cache_v, cache_logf, state_delta, state_conv, page_table, c_prompt, c_sample,
           norm_g, w_ada, b_ada, w_in, conv_w, a_log, dt_bias, b_f, onorm_g, w_o, w_gu, w_down):
    B, L, D = x_prompt.shape
    BS = x_sample.shape[0]
    depth = w_in.shape[0]
    H = D // HD
    KV = cache_k.shape[3]
    KVW = KV * HD
    F = w_down.shape[1]
    c_main = 3 * D + D + 2 * KVW
    c_gate = c_main + 3 * H
    q_col, k_col, v_col = 3 * D, 4 * D, 4 * D + KVW
    assert 4 * H <= LANES and w_in.shape[2] == c_gate + 2 * D

    n_c = B + BS
    pad = (-n_c) % SUBLANES
    c_all = jnp.concatenate([c_prompt, c_sample, jnp.zeros((pad, D), F32)], axis=0)
    mod = _adaln(c_all, w_ada, b_ada)

    w_in_t = jnp.swapaxes(w_in, 1, 2)
    zl = jnp.zeros((depth, H), F32)
    prm = jnp.stack([jnp.concatenate([zl, a_log, zl], 1), jnp.concatenate([zl, dt_bias, zl], 1),
                     jnp.concatenate([zl, zl, b_f], 1)], axis=1)
    prm = jnp.pad(prm, ((0, 0), (0, SUBLANES - 3), (0, LANES - 3 * H)))
    cache_logf_t = jnp.swapaxes(cache_logf, 2, 3)
    w_down_b = w_down.astype(BF16)

    xp, xs = x_prompt, x_sample.reshape(1, BS, D)
    st_p = ([], [], [], [], [])
    st_s = ([], [], [], [], [])
    hp = hs = None
    for l in range(depth):
        mp = [m.reshape(B, 1, D) for m in _split6(mod[l, :B])]
        ms = [m.reshape(1, BS, D) for m in _split6(mod[l, B:n_c])]
        if l == 0:
            hp = _prenorm(xp, norm_g[l, 0], mp[1], mp[0])
            hs = _prenorm(xs, norm_g[l, 0], ms[1], ms[0])
        onorm = onorm_g[l].reshape(1, HD)

        win = functools.partial(_linear, hp.reshape(B * L, D), hs.reshape(BS, D), w_in_t, l, nt=True)
        z, z_s = win(start=0, size=c_main)
        zs, zs_s = win(start=c_main, size=LANES)
        zg, zg_s = win(start=c_gate, size=2 * D)

        z = z.reshape(B, L, c_main)
        gp = _gates(zs.reshape(B, L, LANES), prm[l], H, True)
        gpt = jnp.swapaxes(gp, 1, 2)
        o_a, s_fin = _delta_prompt(z, gp, gpt, conv_w, onorm, l, H)
        m = _fox_prompt(z, gpt, o_a, zg.reshape(B, L, 2 * D), H, KV, q_col, k_col, v_col)
        st_p[0].append(z[:, :, k_col:k_col + KVW].reshape(B, L, KV, HD))
        st_p[1].append(z[:, :, v_col:v_col + KVW].reshape(B, L, KV, HD))
        st_p[2].append(gp[:, :, 2 * H:3 * H])
        st_p[3].append(s_fin)
        st_p[4].append(z[:, L - (CONV_W - 1):, :3 * D])

        gp = _gates(zs_s.reshape(BS, 1, LANES), prm[l], H, False)
        z3 = z_s.reshape(BS, 1, c_main)
        o_a, s_new = _delta_sample(z3, state_conv, state_delta, gp, conv_w, onorm, l, H)
        k_rep = jnp.repeat(z_s[:, k_col:k_col + KVW].reshape(BS, KV, HD), H // KV, axis=1)
        v_rep = jnp.repeat(z_s[:, v_col:v_col + KVW].reshape(BS, KV, HD), H // KV, axis=1)
        o_b = _fox_sample(z_s[:, q_col:q_col + D].reshape(BS, H, HD), k_rep, v_rep, gp, cache_k, cache_v,
                          cache_logf_t, page_table, l, H, KV)
        m_s = _merge(o_a.reshape(1, BS, D), o_b.reshape(1, BS, D), zg_s.reshape(1, BS, 2 * D))
        st_s[0].append(z_s[:, k_col:k_col + KVW].reshape(BS, 1, KV, HD))
        st_s[1].append(z_s[:, v_col:v_col + KVW].reshape(BS, 1, KV, HD))
        st_s[2].append(gp[:, :, 2 * H:3 * H])
        st_s[3].append(s_new)
        st_s[4].append(jnp.concatenate([state_conv[:, l, 1:], z3[:, :, :3 * D]], axis=1))

        y, y_s = _linear(m.reshape(B * L, D), m_s.reshape(BS, D), w_o, l)
        xp, hf = _resnorm(xp, y.reshape(B, L, D), norm_g[l, 1], mp[2], norm_g[l, 2], mp[4], mp[3])
        xs, hf_s = _resnorm(xs, y_s.reshape(1, BS, D), norm_g[l, 1], ms[2], norm_g[l, 2], ms[4], ms[3])
        act, act_s = _linear(hf.reshape(B * L, D), hf_s.reshape(BS, D), w_gu, l, swiglu=True, out_dtype=BF16, tn=256)
        y, y_s = _linear(act, act_s, w_down_b, l, tm=512)
        y, y_s = y.reshape(B, L, D), y_s.reshape(1, BS, D)
        if l + 1 < depth:
            mnext = [mm.reshape(B, 1, D) for mm in _split6(mod[l + 1, :B])]
            xp, hp = _resnorm(xp, y, norm_g[l, 3], mp[5], norm_g[l + 1, 0], mnext[1], mnext[0])
            mnext = [mm.reshape(1, BS, D) for mm in _split6(mod[l + 1, B:n_c])]
            xs, hs = _resnorm(xs, y_s, norm_g[l, 3], ms[5], norm_g[l + 1, 0], mnext[1], mnext[0])
        else:
            xp, _ = _resnorm(xp, y, norm_g[l, 3], mp[5])
            xs, _ = _resnorm(xs, y_s, norm_g[l, 3], ms[5])

    k_p, v_p, lf_p, d_p, cv_p = (jnp.stack(lst, axis=1) for lst in st_p)
    k_s, v_s, lf_s, d_s, cv_s = (jnp.stack(lst, axis=1) for lst in st_s)
    return (xp, xs.reshape(BS, 1, D), k_p, v_p, lf_p, d_p, cv_p, k_s, v_s, lf_s, d_s, cv_s)
```

```python
import functools

import jax
import jax.numpy as jnp
from jax import lax
from jax.experimental import pallas as pl
from jax.experimental.pallas import tpu as pltpu

HD = 128
CONV_W = 4
CHUNK = 64
NORM_EPS = 1e-6
L2_EPS = 1e-6
LOG2E = 1.4426950408889634
LANES = 128
SUBLANES = 8
VMEM_LIMIT = 56 * 1024 * 1024
HI = lax.Precision.HIGHEST
F32 = jnp.float32
BF16 = jnp.bfloat16


def _cparams(n_axes):
    return pltpu.CompilerParams(dimension_semantics=("arbitrary",) * n_axes, vmem_limit_bytes=VMEM_LIMIT)


def _pick(n, pref, align):
    if n <= pref:
        return n
    t = (pref // align) * align
    while t > align and n % t:
        t -= align
    assert n % t == 0, (n, pref, align)
    return t


def _silu(x):
    return x * jax.nn.sigmoid(x)


def _softplus(x):
    return jnp.maximum(x, 0.0) + jnp.log1p(jnp.exp(-jnp.abs(x)))


def _rms(x, g):
    return x * lax.rsqrt(jnp.mean(x * x, axis=-1, keepdims=True) + NORM_EPS) * g


def _dot_nt(a, b, **kw):
    return lax.dot_general(a, b, (((1,), (1,)), ((), ())), preferred_element_type=F32, **kw)


def _dot_tn(a, b, **kw):
    return lax.dot_general(a, b, (((0,), (0,)), ((), ())), preferred_element_type=F32, **kw)


def _linear_kernel(x_ref, xs_ref, *refs, n_w, nt, cast, swiglu):
    w_refs, (o_ref, os_ref), wb_refs = refs[:n_w], refs[n_w:n_w + 2], refs[n_w + 2:]
    first = pl.program_id(1) == 0
    if cast:
        @pl.when(first)
        def _():
            for w_ref, wb_ref in zip(w_refs, wb_refs):
                wb_ref[...] = w_ref[0].astype(BF16)

    def apply(x):
        ws = [wb_ref[...] for wb_ref in wb_refs] if cast else [w_ref[0] for w_ref in w_refs]
        ys = [_dot_nt(x, w) if nt else jnp.dot(x, w, preferred_element_type=F32) for w in ws]
        return _silu(ys[0]) * ys[1] if swiglu else ys[0]

    o_ref[...] = apply(x_ref[...]).astype(o_ref.dtype)

    @pl.when(first)
    def _():
        os_ref[...] = apply(xs_ref[...]).astype(os_ref.dtype)


def _linear(x, xs, w, layer, *, nt=False, start=0, size=None, swiglu=False, out_dtype=F32, tm=1024, tn=512):
    M, K = x.shape
    Ms = xs.shape[0]
    n_all = w.shape[1] if nt else w.shape[2]
    if swiglu:
        assert not nt and start == 0 and size is None
        size = n_all // 2
    size = n_all - start if size is None else size
    tm = _pick(M, tm, SUBLANES)
    tn = _pick(size, tn, LANES)
    nb = size // tn
    cast = w.dtype != BF16

    def w_spec(extra):
        if not nt:
            assert start % tn == 0
            return pl.BlockSpec((1, K, tn), lambda j, i: (layer, 0, start // tn + extra + j))
        if start % tn == 0:
            return pl.BlockSpec((1, tn, K), lambda j, i: (layer, start // tn + j, 0))
        assert start % SUBLANES == 0
        return pl.BlockSpec((pl.Element(1), pl.Element(tn), pl.Element(K)),
                            lambda j, i: (layer, pl.multiple_of(start + j * tn, SUBLANES), 0))

    n_w = 2 if swiglu else 1
    w_specs = [w_spec(0), w_spec(nb)][:n_w]
    w_block = (tn, K) if nt else (K, tn)
    return pl.pallas_call(
        functools.partial(_linear_kernel, n_w=n_w, nt=nt, cast=cast, swiglu=swiglu),
        grid=(nb, M // tm),
        in_specs=[pl.BlockSpec((tm, K), lambda j, i: (i, 0)), pl.BlockSpec((Ms, K), lambda j, i: (0, 0))] + w_specs,
        out_specs=[pl.BlockSpec((tm, tn), lambda j, i: (i, j)), pl.BlockSpec((Ms, tn), lambda j, i: (0, j))],
        out_shape=[jax.ShapeDtypeStruct((M, size), out_dtype), jax.ShapeDtypeStruct((Ms, size), out_dtype)],
        scratch_shapes=[pltpu.VMEM(w_block, BF16)] * (n_w if cast else 0),
        compiler_params=_cparams(2),
        name="linear",
    )(x, xs, *([w] * n_w))


def _adaln_kernel(c_ref, w_ref, b_ref, o_ref):
    a = _silu(c_ref[...]).astype(BF16)
    o_ref[0] = jnp.dot(a, w_ref[0].astype(BF16), preferred_element_type=F32) + b_ref[0]


def _adaln(c_all, w_ada, b_ada):
    depth, D, N = w_ada.shape
    R = c_all.shape[0]
    tn = _pick(N, 512, LANES)
    return pl.pallas_call(
        _adaln_kernel,
        grid=(depth, N // tn),
        in_specs=[pl.BlockSpec((R, D), lambda l, j: (0, 0)),
                  pl.BlockSpec((1, D, tn), lambda l, j: (l, 0, j)),
                  pl.BlockSpec((1, 1, tn), lambda l, j: (l, 0, j))],
        out_specs=pl.BlockSpec((1, R, tn), lambda l, j: (l, 0, j)),
        out_shape=jax.ShapeDtypeStruct((depth, R, N), F32),
        compiler_params=_cparams(2),
    )(c_all, w_ada, b_ada.reshape(depth, 1, N))


def _prenorm_kernel(x_ref, g_ref, sc_ref, sh_ref, h_ref):
    h_ref[0] = (_rms(x_ref[0], g_ref[...]) * (1.0 + sc_ref[0]) + sh_ref[0]).astype(h_ref.dtype)


def _resnorm_kernel(x_ref, y_ref, gpost_ref, gt_ref, gpre_ref, sc_ref, sh_ref, xo_ref, h_ref):
    xn = x_ref[0] + gt_ref[0] * _rms(y_ref[0], gpost_ref[...])
    xo_ref[0] = xn
    h_ref[0] = (_rms(xn, gpre_ref[...]) * (1.0 + sc_ref[0]) + sh_ref[0]).astype(h_ref.dtype)


def _res_kernel(x_ref, y_ref, gpost_ref, gt_ref, xo_ref):
    xo_ref[0] = x_ref[0] + gt_ref[0] * _rms(y_ref[0], gpost_ref[...])


def _row_specs(B, T, D, tl, n_mod):
    act = pl.BlockSpec((1, tl, D), lambda b, t: (b, t, 0))
    gain = pl.BlockSpec((1, D), lambda b, t: (0, 0))
    return act, gain


def _mod_spec(m, tl):
    S = m.shape[1]
    if S == 1:
        return pl.BlockSpec((1, 1, m.shape[2]), lambda b, t: (b, 0, 0))
    return pl.BlockSpec((1, tl, m.shape[2]), lambda b, t: (b, t, 0))


def _prenorm(x, g, sc, sh):
    B, T, D = x.shape
    tl = _pick(T, 256, SUBLANES)
    act, gain = _row_specs(B, T, D, tl, 2)
    return pl.pallas_call(
        _prenorm_kernel,
        grid=(B, T // tl),
        in_specs=[act, gain, _mod_spec(sc, tl), _mod_spec(sh, tl)],
        out_specs=act,
        out_shape=jax.ShapeDtypeStruct((B, T, D), BF16),
        compiler_params=_cparams(2),
    )(x, g.reshape(1, D), sc, sh)


def _resnorm(x, y, gpost, gt, gpre=None, sc=None, sh=None):
    B, T, D = x.shape
    tl = _pick(T, 256, SUBLANES)
    act, gain = _row_specs(B, T, D, tl, 3)
    if gpre is None:
        return pl.pallas_call(
            _res_kernel,
            grid=(B, T // tl),
            in_specs=[act, act, gain, _mod_spec(gt, tl)],
            out_specs=act,
            out_shape=jax.ShapeDtypeStruct((B, T, D), F32),
            compiler_params=_cparams(2),
        )(x, y, gpost.reshape(1, D), gt), None
    return pl.pallas_call(
        _resnorm_kernel,
        grid=(B, T // tl),
        in_specs=[act, act, gain, _mod_spec(gt, tl), gain, _mod_spec(sc, tl), _mod_spec(sh, tl)],
        out_specs=[act, act],
        out_shape=[jax.ShapeDtypeStruct((B, T, D), F32), jax.ShapeDtypeStruct((B, T, D), BF16)],
        compiler_params=_cparams(2),
    )(x, y, gpost.reshape(1, D), gt, gpre.reshape(1, D), sc, sh)


def _merge_kernel(oa_ref, ob_ref, ga_ref, gb_ref, m_ref):
    m = jax.nn.sigmoid(ga_ref[0]) * oa_ref[0] + jax.nn.sigmoid(gb_ref[0]) * ob_ref[0]
    m_ref[0] = m.astype(m_ref.dtype)


def _merge(o_a, o_b, zg):
    B, T, W = o_a.shape
    tl = _pick(T, 256, SUBLANES)
    tc = _pick(W, 1024, LANES)
    nc = W // tc
    spec = pl.BlockSpec((1, tl, tc), lambda b, t, c: (b, t, c))
    return pl.pallas_call(
        _merge_kernel,
        grid=(B, T // tl, nc),
        in_specs=[spec, spec, spec, pl.BlockSpec((1, tl, tc), lambda b, t, c: (b, t, nc + c))],
        out_specs=spec,
        out_shape=jax.ShapeDtypeStruct((B, T, W), BF16),
        compiler_params=_cparams(3),
    )(o_a, o_b, zg, zg)


def _gates_kernel(z_ref, prm_ref, o_ref, carry_ref, *, H, cumulative):
    z = z_ref[0]
    tl = z.shape[0]
    a_log, dt_bias, b_f = prm_ref[0:1, :], prm_ref[1:2, :], prm_ref[2:3, :]
    lane = lax.broadcasted_iota(jnp.int32, z.shape, 1)
    beta = jax.nn.sigmoid(z)
    g = -jnp.exp(a_log) * _softplus(z + dt_bias)
    lf = -_softplus(-(z + b_f))
    if cumulative:
        row = lax.broadcasted_iota(jnp.int32, (tl, tl), 0)
        col = lax.broadcasted_iota(jnp.int32, (tl, tl), 1)
        tri = row >= col
        in_chunk = tri & ((row // CHUNK) == (col // CHUNK))
        g = jnp.dot(in_chunk.astype(F32), g, precision=HI, preferred_element_type=F32)

        @pl.when(pl.program_id(1) == 0)
        def _():
            carry_ref[...] = jnp.zeros_like(carry_ref)

        c = jnp.dot(tri.astype(F32), lf, precision=HI, preferred_element_type=F32) + carry_ref[...]
        carry_ref[...] = c[tl - 1:tl, :]
    else:
        c = lf
    c = pltpu.roll(c, H, axis=1)
    o_ref[0] = jnp.where(lane < H, beta, jnp.where(lane < 2 * H, g, jnp.where(lane < 3 * H, lf, c)))


def _gates(zs, prm, H, cumulative):
    B, T, _ = zs.shape
    tl = _pick(T, 256, CHUNK) if cumulative else T
    spec = pl.BlockSpec((1, tl, LANES), lambda b, t: (b, t, 0))
    return pl.pallas_call(
        functools.partial(_gates_kernel, H=H, cumulative=cumulative),
        grid=(B, T // tl),
        in_specs=[spec, pl.BlockSpec((SUBLANES, LANES), lambda b, t: (0, 0))],
        out_specs=spec,
        out_shape=jax.ShapeDtypeStruct((B, T, LANES), F32),
        scratch_shapes=[pltpu.VMEM((1, LANES), F32)],
        compiler_params=_cparams(2),
    )(zs, prm)


def _lane_col(x, idx):
    lane = lax.broadcasted_iota(jnp.int32, x.shape, 1)
    return jnp.sum(jnp.where(lane == idx, x, 0.0), axis=1, keepdims=True)


def _bf16_dot(a, b):
    return jnp.dot(a.astype(BF16), b.astype(BF16), preferred_element_type=F32)


def _split_bf16(x):
    hi = x.astype(BF16)
    return hi, (x - hi.astype(F32)).astype(BF16)


def _dot_3pass(a, b):
    a_hi, a_lo = _split_bf16(a)
    b_hi, b_lo = _split_bf16(b)
    dot = functools.partial(jnp.dot, preferred_element_type=F32)
    return dot(a_hi, b_hi) + (dot(a_hi, b_lo) + dot(a_lo, b_hi))


def _each(f, *lists):
    return [f(*args) for args in zip(*lists)]


def _unit_lower_inverse(a_list, dot):
    n = a_list[0].shape[0]
    row = lax.broadcasted_iota(jnp.int32, (n, n), 0)
    col = lax.broadcasted_iota(jnp.int32, (n, n), 1)
    eye = (row == col).astype(F32)
    base = SUBLANES
    diag = (row // base) == (col // base)
    ad = _each(lambda a: jnp.where(diag, a, 0.0), a_list)
    a2 = _each(lambda x: dot(x, x), ad)
    a4 = _each(lambda x: dot(x, x), a2)
    t = _each(lambda x, y: dot(eye - x, eye + y), ad, a2)
    t = _each(lambda x, y: dot(x, eye + y), t, a4)
    s = base
    while s < n:
        off = ((row // (2 * s)) == (col // (2 * s))) & ((row // s) != (col // s))
        tm = _each(lambda x, a: dot(x, jnp.where(off, a, 0.0)), t, a_list)
        t = _each(lambda x, y: x - dot(y, x), t, tm)
        s *= 2
    return t


def _unit_lower_solve(a_list, rhs_list):
    t0 = _each(lambda t: t.astype(BF16), _unit_lower_inverse(a_list, _bf16_dot))
    x0 = _each(lambda t, b: jnp.dot(t, b.astype(BF16), preferred_element_type=F32), t0, rhs_list)
    r = _each(lambda a, b, x: b - x - _dot_3pass(a, x), a_list, rhs_list, x0)
    return _each(lambda t, x, rr: x + jnp.dot(t, rr.astype(BF16), preferred_element_type=F32), t0, x0, r)


def _conv_rows(xb, i, w_ref, n):
    ext = xb[i, 0:SUBLANES + n, :]
    acc = ext[SUBLANES:] * w_ref[0, CONV_W - 1:CONV_W, :]
    for s in range(1, CONV_W):
        back = pltpu.roll(ext, s, axis=0)[SUBLANES:]
        acc = acc + back * w_ref[0, CONV_W - 1 - s:CONV_W - s, :]
    return acc


def _l2n(t):
    return t * lax.rsqrt(jnp.sum(t * t, axis=-1, keepdims=True) + L2_EPS)


def _delta_prompt_kernel(zq_ref, zk_ref, zv_ref, wq_ref, wk_ref, wv_ref, gp_ref, gpt_ref, on_ref,
                         o_ref, sfin_ref, s_ref, xb, *, H, HB):
    hg = pl.program_id(1)
    t = pl.program_id(2)
    Lb = zq_ref.shape[1]

    @pl.when(t == 0)
    def _():
        s_ref[...] = jnp.zeros_like(s_ref)
        xb[:, 0:SUBLANES, :] = jnp.zeros((3, SUBLANES, HB * HD), F32)

    @pl.when(t > 0)
    def _():
        xb[:, 0:SUBLANES, :] = xb[:, Lb:Lb + SUBLANES, :]

    xb[0, SUBLANES:SUBLANES + Lb, :] = zq_ref[0]
    xb[1, SUBLANES:SUBLANES + Lb, :] = zk_ref[0]
    xb[2, SUBLANES:SUBLANES + Lb, :] = zv_ref[0]
    q_all = _silu(_conv_rows(xb, 0, wq_ref, Lb))
    k_all = _silu(_conv_rows(xb, 1, wk_ref, Lb))
    v_all = _silu(_conv_rows(xb, 2, wv_ref, Lb))

    gp = gp_ref[0]
    row = lax.broadcasted_iota(jnp.int32, (CHUNK, CHUNK), 0)
    col = lax.broadcasted_iota(jnp.int32, (CHUNK, CHUNK), 1)
    nc = Lb // CHUNK
    decay, kq_lhs, k_rhs, rhs_l, qg_l, kd_l, gl_l = [], [], [], [], [], [], []
    for j in range(HB):
        head = hg * HB + j
        hs = slice(j * HD, (j + 1) * HD)
        q = _l2n(q_all[:, hs]) * (HD ** -0.5)
        k = _l2n(k_all[:, hs])
        v = v_all[:, hs]
        beta = _lane_col(gp, head)
        gcum = _lane_col(gp, head + H)
        grow = gpt_ref[0, pl.ds(head + H, 1), :]
        eg = jnp.exp(gcum)
        kb = k * beta
        rhs = jnp.concatenate([v * beta, kb * eg], axis=1)
        qg = q * eg
        for c in range(nc):
            sl = slice(c * CHUNK, (c + 1) * CHUNK)
            gc = gcum[sl]
            g_last = gc[CHUNK - 1:CHUNK, :]
            decay.append(jnp.exp(jnp.where(row >= col, gc - grow[:, sl], -jnp.inf)))
            kq_lhs.append(jnp.concatenate([kb[sl], q[sl]], axis=0).astype(BF16))
            k_rhs.append(k[sl].astype(BF16))
            rhs_l.append(rhs[sl])
            qg_l.append(qg[sl])
            kd_l.append((k[sl] * jnp.exp(g_last - gc)).astype(BF16))
            gl_l.append(jnp.exp(g_last))
    kq = _each(_dot_nt, kq_lhs, k_rhs)
    a_mat = _each(lambda x, d: jnp.where(row > col, x[:CHUNK] * d, 0.0), kq, decay)
    a_qk = _each(lambda x, d: (x[CHUNK:] * d).astype(BF16), kq, decay)
    sol = _unit_lower_solve(a_mat, rhs_l)
    wq = _each(lambda x, g: jnp.concatenate([x[:, HD:], g], axis=0).astype(BF16), sol, qg_l)

    s = [s_ref[j] for j in range(HB)]
    outs = [[] for _ in range(HB)]
    for c in range(nc):
        idx = [j * nc + c for j in range(HB)]
        ws_qs = [jnp.dot(wq[i], s[j].astype(BF16), preferred_element_type=F32) for j, i in enumerate(idx)]
        db = [(sol[i][:, :HD] - ws_qs[j][:CHUNK]).astype(BF16) for j, i in enumerate(idx)]
        for j, i in enumerate(idx):
            outs[j].append(ws_qs[j][CHUNK:] + jnp.dot(a_qk[i], db[j], preferred_element_type=F32))
        s = [gl_l[i] * s[j] + _dot_tn(kd_l[i], db[j]) for j, i in enumerate(idx)]
    for j in range(HB):
        s_ref[j] = s[j]
        o_ref[0, :, j * HD:(j + 1) * HD] = _rms(jnp.concatenate(outs[j], axis=0), on_ref[...])

    @pl.when(t == pl.num_programs(2) - 1)
    def _():
        sfin_ref[0] = s_ref[...]


def _delta_prompt(z, gp, gpt, conv_w, onorm, layer, H, HB=8):
    B, L, _ = z.shape
    Lb = _pick(L, 256, CHUNK)
    HB = HB if H % HB == 0 else 1
    ng = H // HB

    def zspec(off):
        return pl.BlockSpec((1, Lb, HB * HD), lambda b, h, t: (b, t, off + h))

    def wspec(off):
        return pl.BlockSpec((1, CONV_W, HB * HD), lambda b, h, t: (layer, 0, off + h))

    return pl.pallas_call(
        functools.partial(_delta_prompt_kernel, H=H, HB=HB),
        grid=(B, ng, L // Lb),
        in_specs=[zspec(0), zspec(ng), zspec(2 * ng), wspec(0), wspec(ng), wspec(2 * ng),
                  pl.BlockSpec((1, Lb, LANES), lambda b, h, t: (b, t, 0)),
                  pl.BlockSpec((1, LANES, Lb), lambda b, h, t: (b, 0, t)),
                  pl.BlockSpec((1, HD), lambda b, h, t: (0, 0))],
        out_specs=[pl.BlockSpec((1, Lb, HB * HD), lambda b, h, t: (b, t, h)),
                   pl.BlockSpec((1, HB, HD, HD), lambda b, h, t: (b, h, 0, 0))],
        out_shape=[jax.ShapeDtypeStruct((B, L, H * HD), F32), jax.ShapeDtypeStruct((B, H, HD, HD), F32)],
        scratch_shapes=[pltpu.VMEM((HB, HD, HD), F32), pltpu.VMEM((3, Lb + SUBLANES, HB * HD), F32)],
        compiler_params=_cparams(3),
        name="delta_prompt",
    )(z, z, z, conv_w, conv_w, conv_w, gp, gpt, onorm)


def _delta_sample_kernel(zq_ref, zk_ref, zv_ref, bq_ref, bk_ref, bv_ref, wq_ref, wk_ref, wv_ref, gp_ref, on_ref,
                         s0_ref, o_ref, s_ref, *, H, HB):
    hg = pl.program_id(1)

    def conv(z_ref, b_ref, w_ref):
        acc = z_ref[0] * w_ref[0, CONV_W - 1:CONV_W, :]
        for j in range(CONV_W - 1):
            acc = acc + b_ref[0, 0, j:j + 1, :] * w_ref[0, j:j + 1, :]
        return _silu(acc)

    q_all = conv(zq_ref, bq_ref, wq_ref)
    k_all = conv(zk_ref, bk_ref, wk_ref)
    v_all = conv(zv_ref, bv_ref, wv_ref)
    gp = gp_ref[0]
    hs = [slice(j * HD, (j + 1) * HD) for j in range(HB)]
    q = [_l2n(q_all[:, sl]) * (HD ** -0.5) for sl in hs]
    k = [_l2n(k_all[:, sl]) for sl in hs]
    v = [v_all[:, sl] for sl in hs]
    beta = [_lane_col(gp, hg * HB + j) for j in range(HB)]
    g = [_lane_col(gp, hg * HB + j + H) for j in range(HB)]
    dot = functools.partial(jnp.dot, precision=HI, preferred_element_type=F32)
    row = lax.broadcasted_iota(jnp.int32, (HD, HD), 0)
    s = [jnp.exp(g[j]) * s0_ref[0, 0, j] for j in range(HB)]
    ks = _each(lambda kk, ss: dot(jnp.broadcast_to(kk, (SUBLANES, HD)), ss)[0:1], k, s)
    delta = _each(lambda b, vv, x: b * (vv - x), beta, v, ks)
    outer = _each(lambda kk, dd: _dot_tn(jnp.where(row == 0, jnp.broadcast_to(kk, (HD, HD)), 0.0),
                                         jnp.broadcast_to(dd, (HD, HD)), precision=HI), k, delta)
    s = _each(lambda a, b: a + b, s, outer)
    o = _each(lambda qq, ss: dot(jnp.broadcast_to(qq, (SUBLANES, HD)), ss)[0:1], q, s)
    for j in range(HB):
        s_ref[0, j] = s[j]
        o_ref[0, :, hs[j]] = _rms(o[j], on_ref[...])


def _delta_sample(z3, state_conv, state_delta, gp, conv_w, onorm, layer, H, HB=8):
    B = z3.shape[0]
    HB = HB if H % HB == 0 else 1
    ng = H // HB

    def zspec(off):
        return pl.BlockSpec((1, 1, HB * HD), lambda b, h: (b, 0, off + h))

    def bspec(off):
        return pl.BlockSpec((1, 1, CONV_W - 1, HB * HD), lambda b, h: (b, layer, 0, off + h))

    def wspec(off):
        return pl.BlockSpec((1, CONV_W, HB * HD), lambda b, h: (layer, 0, off + h))

    return pl.pallas_call(
        functools.partial(_delta_sample_kernel, H=H, HB=HB),
        grid=(B, ng),
        in_specs=[zspec(0), zspec(ng), zspec(2 * ng), bspec(0), bspec(ng), bspec(2 * ng),
                  wspec(0), wspec(ng), wspec(2 * ng),
                  pl.BlockSpec((1, 1, LANES), lambda b, h: (b, 0, 0)),
                  pl.BlockSpec((1, HD), lambda b, h: (0, 0)),
                  pl.BlockSpec((1, 1, HB, HD, HD), lambda b, h: (b, layer, h, 0, 0))],
        out_specs=[pl.BlockSpec((1, 1, HB * HD), lambda b, h: (b, 0, h)),
                   pl.BlockSpec((1, HB, HD, HD), lambda b, h: (b, h, 0, 0))],
        out_shape=[jax.ShapeDtypeStruct((B, 1, H * HD), F32), jax.ShapeDtypeStruct((B, H, HD, HD), F32)],
        compiler_params=_cparams(2),
        name="delta_sample",
    )(z3, z3, z3, state_conv, state_conv, state_conv, conv_w, conv_w, conv_w, gp, onorm, state_delta)


def _fox_prompt_kernel(q_ref, k_ref, v_ref, gpt_ref, oa_ref, ga_ref, gb_ref, o_ref, kb_ref, va_ref, *, H, G):
    kvh = pl.program_id(1)
    qi = pl.program_id(2)
    tq = q_ref.shape[1]
    L = k_ref.shape[1]

    @pl.when(qi == 0)
    def _():
        kb_ref[...] = k_ref[0].astype(BF16)
        va_ref[:, :HD] = v_ref[0].astype(BF16)
        va_ref[:, HD:] = jnp.ones((L, HD), BF16)

    causal = lax.broadcasted_iota(jnp.int32, (tq, tq), 1) <= lax.broadcasted_iota(jnp.int32, (tq, tq), 0)
    scale = HD ** -0.5 * LOG2E
    dot = functools.partial(jnp.dot, preferred_element_type=F32)

    def attend(lo):
        for g in range(G):
            sl = slice(g * HD, (g + 1) * HD)
            qb = q_ref[0, :, sl].astype(BF16)
            ck = gpt_ref[0, pl.ds(3 * H + kvh * G + g, 1), :] * LOG2E
            s_own = jnp.where(causal, _dot_nt(qb, kb_ref[lo:lo + tq, :]) * scale - ck[:, lo:lo + tq], -jnp.inf)
            m = jnp.max(s_own, axis=1, keepdims=True)
            if lo:
                s_past = _dot_nt(qb, kb_ref[0:lo, :]) * scale - ck[:, 0:lo]
                m = jnp.maximum(m, jnp.max(s_past, axis=1, keepdims=True))
            pv = dot(jnp.exp2(s_own - m).astype(BF16), va_ref[lo:lo + tq, :])
            if lo:
                pv = pv + dot(jnp.exp2(s_past - m).astype(BF16), va_ref[0:lo, :])
            o_b = pv[:, :HD] / pv[:, HD:]
            o_ref[0, :, sl] = (jax.nn.sigmoid(ga_ref[0, :, sl]) * oa_ref[0, :, sl]
                               + jax.nn.sigmoid(gb_ref[0, :, sl]) * o_b).astype(o_ref.dtype)

    for n in range(L // tq):
        pl.when(qi == n)(functools.partial(attend, n * tq))


def _fox_prompt(z, gpt, o_a, zg, H, KV, q_col, k_col, v_col):
    B, L, _ = z.shape
    D = H * HD
    G = H // KV
    tq = _pick(L, 512, LANES)
    qw = G * HD
    blk = pl.BlockSpec((1, tq, qw), lambda b, kv, qi: (b, qi, kv))
    return pl.pallas_call(
        functools.partial(_fox_prompt_kernel, H=H, G=G),
        grid=(B, KV, L // tq),
        in_specs=[pl.BlockSpec((1, tq, qw), lambda b, kv, qi: (b, qi, q_col // qw + kv)),
                  pl.BlockSpec((1, L, HD), lambda b, kv, qi: (b, 0, k_col // HD + kv)),
                  pl.BlockSpec((1, L, HD), lambda b, kv, qi: (b, 0, v_col // HD + kv)),
                  pl.BlockSpec((1, LANES, L), lambda b, kv, qi: (b, 0, 0)),
                  blk, blk,
                  pl.BlockSpec((1, tq, qw), lambda b, kv, qi: (b, qi, D // qw + kv))],
        out_specs=blk,
        out_shape=jax.ShapeDtypeStruct((B, L, D), BF16),
        scratch_shapes=[pltpu.VMEM((L, HD), BF16), pltpu.VMEM((L, 2 * HD), BF16)],
        compiler_params=_cparams(3),
        name="fox_prompt",
    )(z, z, z, gpt, o_a, zg, zg)


def _fox_sample_kernel(pt_ref, q_ref, kn_ref, vn_ref, gp_ref, *refs, H, G, PP):
    k_refs, v_refs, lf_refs = refs[:PP], refs[PP:2 * PP], refs[2 * PP:3 * PP]
    o_ref, m_ref, l_ref, acc_ref, carry_ref, later_ref = refs[3 * PP:]
    p = pl.program_id(1)
    KV = H // G
    P = k_refs[0].shape[0]
    W = P * KV

    @pl.when(p == 0)
    def _():
        m_ref[...] = jnp.full(m_ref.shape, -jnp.inf, F32)
        l_ref[...] = jnp.zeros_like(l_ref)
        acc_ref[...] = jnp.zeros_like(acc_ref)
        r2 = lax.broadcasted_iota(jnp.int32, (H, LANES), 0)
        c2 = lax.broadcasted_iota(jnp.int32, (H, LANES), 1)
        lf_new = jnp.broadcast_to(gp_ref[0], (H, LANES))
        carry_ref[...] = jnp.sum(jnp.where(c2 == r2 + 2 * H, lf_new, 0.0), axis=1, keepdims=True)
        later_ref[...] = (lax.broadcasted_iota(jnp.int32, (P, W), 0)
                          > lax.broadcasted_iota(jnp.int32, (P, W), 1) // KV).astype(BF16)

    own = (lax.rem(lax.broadcasted_iota(jnp.int32, (H, W), 1), KV)
           == lax.broadcasted_iota(jnp.int32, (H, W), 0) // G)
    qb = q_ref[0].astype(BF16)
    later = later_ref[...]
    dot = functools.partial(jnp.dot, preferred_element_type=F32)
    lf = [lf_refs[i][...] for i in range(PP)]
    lf_hi = _each(lambda x: x.astype(BF16), lf)
    lf_r1 = _each(lambda x, h: x - h.astype(F32), lf, lf_hi)
    lf_mid = _each(lambda x: x.astype(BF16), lf_r1)
    lf_lo = _each(lambda x, m: (x - m.astype(F32)).astype(BF16), lf_r1, lf_mid)
    bias = _each(lambda a, b, c: dot(a, later) + (dot(b, later) + dot(c, later)), lf_hi, lf_mid, lf_lo)
    qk = [_dot_nt(qb, k_refs[i][...].reshape(W, HD).astype(BF16)) for i in range(PP)]
    carry = carry_ref[...]
    s = []
    for i in range(PP):
        s.append(jnp.where(own, qk[i] * (HD ** -0.5) + (bias[i] + carry), -jnp.inf))
        carry = carry + jnp.sum(lf[i], axis=1, keepdims=True)
    carry_ref[...] = carry
    m_old = m_ref[...]
    m_new = m_old
    for x in s:
        m_new = jnp.maximum(m_new, jnp.max(x, axis=1, keepdims=True))
    alpha = jnp.exp(m_old - m_new)
    pr = _each(lambda x: jnp.exp(x - m_new), s)
    l_new = alpha * l_ref[...]
    for x in pr:
        l_new = l_new + jnp.sum(x, axis=1, keepdims=True)
    l_ref[...] = l_new
    pv = [dot(pr[i].astype(BF16), v_refs[i][...].reshape(W, HD).astype(BF16)) for i in range(PP)]
    acc = alpha * acc_ref[...]
    for x in pv:
        acc = acc + x
    acc_ref[...] = acc
    m_ref[...] = m_new

    @pl.when(p == pl.num_programs(1) - 1)
    def _():
        s_new = jnp.sum(q_ref[0] * kn_ref[0], axis=1, keepdims=True) * (HD ** -0.5)
        m_old = m_ref[...]
        m_new = jnp.maximum(m_old, s_new)
        alpha = jnp.exp(m_old - m_new)
        p_new = jnp.exp(s_new - m_new)
        o_ref[0] = (alpha * acc_ref[...] + p_new * vn_ref[0]) / (alpha * l_ref[...] + p_new)


def _fox_sample(q, k_new, v_new, gp, cache_k, cache_v, cache_logf_t, page_table, layer, H, KV, PP=8):
    B, n_pages = page_table.shape
    P = cache_k.shape[2]
    G = H // KV
    while n_pages % PP:
        PP -= 1

    def page_spec(shape, i):
        return pl.BlockSpec((None, None) + shape, lambda b, p, pt: (pt[b, n_pages - 1 - (p * PP + i)], layer, 0, 0, 0)[:2 + len(shape)])

    row = pl.BlockSpec((1, H, HD), lambda b, p, pt: (b, 0, 0))
    grid_spec = pltpu.PrefetchScalarGridSpec(
        num_scalar_prefetch=1,
        grid=(B, n_pages // PP),
        in_specs=([row, row, row, pl.BlockSpec((1, 1, LANES), lambda b, p, pt: (b, 0, 0))]
                  + [page_spec((P, KV, HD), i) for i in range(PP)]
                  + [page_spec((P, KV, HD), i) for i in range(PP)]
                  + [page_spec((H, P), i) for i in range(PP)]),
        out_specs=row,
        scratch_shapes=[pltpu.VMEM((H, 1), F32), pltpu.VMEM((H, 1), F32), pltpu.VMEM((H, HD), F32),
                        pltpu.VMEM((H, 1), F32), pltpu.VMEM((P, P * KV), BF16)],
    )
    return pl.pallas_call(
        functools.partial(_fox_sample_kernel, H=H, G=G, PP=PP),
        grid_spec=grid_spec,
        out_shape=jax.ShapeDtypeStruct((B, H, HD), F32),
        compiler_params=_cparams(2),
        name="fox_sample",
    )(page_table, q, k_new, v_new, gp, *([cache_k] * PP), *([cache_v] * PP), *([cache_logf_t] * PP))


def _split6(m):
    return tuple(jnp.split(m, 6, axis=-1))


def kernel(x_prompt, x_sample, cache_k, cache_v, cache_logf, state_delta, state_conv, page_table, c_prompt, c_sample,
           norm_g, w_ada, b_ada, w_in, conv_w, a_log, dt_bias, b_f, onorm_g, w_o, w_gu, w_down):
    B, L, D = x_prompt.shape
    BS = x_sample.shape[0]
    depth = w_in.shape[0]
    H = D // HD
    KV = cache_k.shape[3]
    KVW = KV * HD
    F = w_down.shape[1]
    c_main = 3 * D + D + 2 * KVW
    c_gate = c_main + 3 * H
    q_col, k_col, v_col = 3 * D, 4 * D, 4 * D + KVW
    assert 4 * H <= LANES and w_in.shape[2] == c_gate + 2 * D

    n_c = B + BS
    pad = (-n_c) % SUBLANES
    c_all = jnp.concatenate([c_prompt, c_sample, jnp.zeros((pad, D), F32)], axis=0)
    mod = _adaln(c_all, w_ada, b_ada)

    w_in_t = jnp.swapaxes(w_in, 1, 2)
    zl = jnp.zeros((depth, H), F32)
    prm = jnp.stack([jnp.concatenate([zl, a_log, zl], 1), jnp.concatenate([zl, dt_bias, zl], 1),
                     jnp.concatenate([zl, zl, b_f], 1)], axis=1)
    prm = jnp.pad(prm, ((0, 0), (0, SUBLANES - 3), (0, LANES - 3 * H)))
    cache_logf_t = jnp.swapaxes(cache_logf, 2, 3)
    w_down_b = w_down.astype(BF16)

    xp, xs = x_prompt, x_sample.reshape(1, BS, D)
    st_p = ([], [], [], [], [])
    st_s = ([], [], [], [], [])
    hp = hs = None
    for l in range(depth):
        mp = [m.reshape(B, 1, D) for m in _split6(mod[l, :B])]
        ms = [m.reshape(1, BS, D) for m in _split6(mod[l, B:n_c])]
        if l == 0:
            hp = _prenorm(xp, norm_g[l, 0], mp[1], mp[0])
            hs = _prenorm(xs, norm_g[l, 0], ms[1], ms[0])
        onorm = onorm_g[l].reshape(1, HD)

        win = functools.partial(_linear, hp.reshape(B * L, D), hs.reshape(BS, D), w_in_t, l, nt=True)
        z, z_s = win(start=0, size=c_main)
        zs, zs_s = win(start=c_main, size=LANES)
        zg, zg_s = win(start=c_gate, size=2 * D)

        z = z.reshape(B, L, c_main)
        gp = _gates(zs.reshape(B, L, LANES), prm[l], H, True)
        gpt = jnp.swapaxes(gp, 1, 2)
        o_a, s_fin = _delta_prompt(z, gp, gpt, conv_w, onorm, l, H)
        m = _fox_prompt(z, gpt, o_a, zg.reshape(B, L, 2 * D), H, KV, q_col, k_col, v_col)
        st_p[0].append(z[:, :, k_col:k_col + KVW].reshape(B, L, KV, HD))
        st_p[1].append(z[:, :, v_col:v_col + KVW].reshape(B, L, KV, HD))
        st_p[2].append(gp[:, :, 2 * H:3 * H])
        st_p[3].append(s_fin)
        st_p[4].append(z[:, L - (CONV_W - 1):, :3 * D])

        gp = _gates(zs_s.reshape(BS, 1, LANES), prm[l], H, False)
        z3 = z_s.reshape(BS, 1, c_main)
        o_a, s_new = _delta_sample(z3, state_conv, state_delta, gp, conv_w, onorm, l, H)
        k_rep = jnp.repeat(z_s[:, k_col:k_col + KVW].reshape(BS, KV, HD), H // KV, axis=1)
        v_rep = jnp.repeat(z_s[:, v_col:v_col + KVW].reshape(BS, KV, HD), H // KV, axis=1)
        o_b = _fox_sample(z_s[:, q_col:q_col + D].reshape(BS, H, HD), k_rep, v_rep, gp, cache_k, cache_v,
                          cache_logf_t, page_table, l, H, KV)
        m_s = _merge(o_a.reshape(1, BS, D), o_b.reshape(1, BS, D), zg_s.reshape(1, BS, 2 * D))
        st_s[0].append(z_s[:, k_col:k_col + KVW].reshape(BS, 1, KV, HD))
        st_s[1].append(z_s[:, v_col:v_col + KVW].reshape(BS, 1, KV, HD))
        st_s[2].append(gp[:, :, 2 * H:3 * H])
        st_s[3].append(s_new)
        st_s[4].append(jnp.concatenate([state_conv[:, l, 1:], z3[:, :, :3 * D]], axis=1))

        y, y_s = _linear(m.reshape(B * L, D), m_s.reshape(BS, D), w_o, l)
        xp, hf = _resnorm(xp, y.reshape(B, L, D), norm_g[l, 1], mp[2], norm_g[l, 2], mp[4], mp[3])
        xs, hf_s = _resnorm(xs, y_s.reshape(1, BS, D), norm_g[l, 1], ms[2], norm_g[l, 2], ms[4], ms[3])
        act, act_s = _linear(hf.reshape(B * L, D), hf_s.reshape(BS, D), w_gu, l, swiglu=True, out_dtype=BF16, tn=256)
        y, y_s = _linear(act, act_s, w_down_b, l, tm=512)
        y, y_s = y.reshape(B, L, D), y_s.reshape(1, BS, D)
        if l + 1 < depth:
            mnext = [mm.reshape(B, 1, D) for mm in _split6(mod[l + 1, :B])]
            xp, hp = _resnorm(xp, y, norm_g[l, 3], mp[5], norm_g[l + 1, 0], mnext[1], mnext[0])
            mnext = [mm.reshape(1, BS, D) for mm in _split6(mod[l + 1, B:n_c])]
            xs, hs = _resnorm(xs, y_s, norm_g[l, 3], ms[5], norm_g[l + 1, 0], mnext[1], mnext[0])
        else:
            xp, _ = _resnorm(xp, y, norm_g[l, 3], mp[5])
            xs, _ = _resnorm(xs, y_s, norm_g[l, 3], ms[5])

    k_p, v_p, lf_p, d_p, cv_p = (jnp.stack(lst, axis=1) for lst in st_p)
    k_s, v_s, lf_s, d_s, cv_s = (jnp.stack(lst, axis=1) for lst in st_s)
    return (xp, xs.reshape(BS, 1, D), k_p, v_p, lf_p, d_p, cv_p, k_s, v_s, lf_s, d_s, cv_s)
```

```python
import functools

import jax
import jax.numpy as jnp
from jax import lax
from jax.experimental import pallas as pl
from jax.experimental.pallas import tpu as pltpu

HD = 128
CONV_W = 4
CHUNK = 64
NORM_EPS = 1e-6
L2_EPS = 1e-6
LOG2E = 1.4426950408889634
LANES = 128
SUBLANES = 8
VMEM_LIMIT = 56 * 1024 * 1024
HI = lax.Precision.HIGHEST
F32 = jnp.float32
BF16 = jnp.bfloat16


def _cparams(n_axes):
    return pltpu.CompilerParams(dimension_semantics=("arbitrary",) * n_axes, vmem_limit_bytes=VMEM_LIMIT)


def _pick(n, pref, align):
    if n <= pref:
        return n
    t = (pref // align) * align
    while t > align and n % t:
        t -= align
    assert n % t == 0, (n, pref, align)
    return t


def _silu(x):
    return x * jax.nn.sigmoid(x)


def _softplus(x):
    return jnp.maximum(x, 0.0) + jnp.log1p(jnp.exp(-jnp.abs(x)))


def _rms(x, g):
    return x * lax.rsqrt(jnp.mean(x * x, axis=-1, keepdims=True) + NORM_EPS) * g


def _dot_nt(a, b, **kw):
    return lax.dot_general(a, b, (((1,), (1,)), ((), ())), preferred_element_type=F32, **kw)


def _dot_tn(a, b, **kw):
    return lax.dot_general(a, b, (((0,), (0,)), ((), ())), preferred_element_type=F32, **kw)


def _linear_kernel(x_ref, xs_ref, *refs, n_w, nt, cast, swiglu):
    w_refs, (o_ref, os_ref), wb_refs = refs[:n_w], refs[n_w:n_w + 2], refs[n_w + 2:]
    first = pl.program_id(1) == 0
    if cast:
        @pl.when(first)
        def _():
            for w_ref, wb_ref in zip(w_refs, wb_refs):
                wb_ref[...] = w_ref[0].astype(BF16)

    def apply(x):
        ws = [wb_ref[...] for wb_ref in wb_refs] if cast else [w_ref[0] for w_ref in w_refs]
        ys = [_dot_nt(x, w) if nt else jnp.dot(x, w, preferred_element_type=F32) for w in ws]
        return _silu(ys[0]) * ys[1] if swiglu else ys[0]

    o_ref[...] = apply(x_ref[...]).astype(o_ref.dtype)

    @pl.when(first)
    def _():
        os_ref[...] = apply(xs_ref[...]).astype(os_ref.dtype)


def _linear(x, xs, w, layer, *, nt=False, start=0, size=None, swiglu=False, out_dtype=F32, tm=1024, tn=512):
    M, K = x.shape
    Ms = xs.shape[0]
    n_all = w.shape[1] if nt else w.shape[2]
    if swiglu:
        assert not nt and start == 0 and size is None
        size = n_all // 2
    size = n_all - start if size is None else size
    tm = _pick(M, tm, SUBLANES)
    tn = _pick(size, tn, LANES)
    nb = size // tn
    cast = w.dtype != BF16

    def w_spec(extra):
        if not nt:
            assert start % tn == 0
            return pl.BlockSpec((1, K, tn), lambda j, i: (layer, 0, start // tn + extra + j))
        if start % tn == 0:
            return pl.BlockSpec((1, tn, K), lambda j, i: (layer, start // tn + j, 0))
        assert start % SUBLANES == 0
        return pl.BlockSpec((pl.Element(1), pl.Element(tn), pl.Element(K)),
                            lambda j, i: (layer, pl.multiple_of(start + j * tn, SUBLANES), 0))

    n_w = 2 if swiglu else 1
    w_specs = [w_spec(0), w_spec(nb)][:n_w]
    w_block = (tn, K) if nt else (K, tn)
    return pl.pallas_call(
        functools.partial(_linear_kernel, n_w=n_w, nt=nt, cast=cast, swiglu=swiglu),
        grid=(nb, M // tm),
        in_specs=[pl.BlockSpec((tm, K), lambda j, i: (i, 0)), pl.BlockSpec((Ms, K), lambda j, i: (0, 0))] + w_specs,
        out_specs=[pl.BlockSpec((tm, tn), lambda j, i: (i, j)), pl.BlockSpec((Ms, tn), lambda j, i: (0, j))],
        out_shape=[jax.ShapeDtypeStruct((M, size), out_dtype), jax.ShapeDtypeStruct((Ms, size), out_dtype)],
        scratch_shapes=[pltpu.VMEM(w_block, BF16)] * (n_w if cast else 0),
        compiler_params=_cparams(2),
        name="linear",
    )(x, xs, *([w] * n_w))


def _adaln_kernel(c_ref, w_ref, b_ref, o_ref):
    a = _silu(c_ref[...]).astype(BF16)
    o_ref[0] = jnp.dot(a, w_ref[0].astype(BF16), preferred_element_type=F32) + b_ref[0]


def _adaln(c_all, w_ada, b_ada):
    depth, D, N = w_ada.shape
    R = c_all.shape[0]
    tn = _pick(N, 512, LANES)
    return pl.pallas_call(
        _adaln_kernel,
        grid=(depth, N // tn),
        in_specs=[pl.BlockSpec((R, D), lambda l, j: (0, 0)),
                  pl.BlockSpec((1, D, tn), lambda l, j: (l, 0, j)),
                  pl.BlockSpec((1, 1, tn), lambda l, j: (l, 0, j))],
        out_specs=pl.BlockSpec((1, R, tn), lambda l, j: (l, 0, j)),
        out_shape=jax.ShapeDtypeStruct((depth, R, N), F32),
        compiler_params=_cparams(2),
    )(c_all, w_ada, b_ada.reshape(depth, 1, N))


def _prenorm_kernel(x_ref, g_ref, sc_ref, sh_ref, h_ref):
    h_ref[0] = (_rms(x_ref[0], g_ref[...]) * (1.0 + sc_ref[0]) + sh_ref[0]).astype(h_ref.dtype)


def _resnorm_kernel(x_ref, y_ref, gpost_ref, gt_ref, gpre_ref, sc_ref, sh_ref, xo_ref, h_ref):
    xn = x_ref[0] + gt_ref[0] * _rms(y_ref[0], gpost_ref[...])
    xo_ref[0] = xn
    h_ref[0] = (_rms(xn, gpre_ref[...]) * (1.0 + sc_ref[0]) + sh_ref[0]).astype(h_ref.dtype)


def _res_kernel(x_ref, y_ref, gpost_ref, gt_ref, xo_ref):
    xo_ref[0] = x_ref[0] + gt_ref[0] * _rms(y_ref[0], gpost_ref[...])


def _row_specs(B, T, D, tl, n_mod):
    act = pl.BlockSpec((1, tl, D), lambda b, t: (b, t, 0))
    gain = pl.BlockSpec((1, D), lambda b, t: (0, 0))
    return act, gain


def _mod_spec(m, tl):
    S = m.shape[1]
    if S == 1:
        return pl.BlockSpec((1, 1, m.shape[2]), lambda b, t: (b, 0, 0))
    return pl.BlockSpec((1, tl, m.shape[2]), lambda b, t: (b, t, 0))


def _prenorm(x, g, sc, sh):
    B, T, D = x.shape
    tl = _pick(T, 256, SUBLANES)
    act, gain = _row_specs(B, T, D, tl, 2)
    return pl.pallas_call(
        _prenorm_kernel,
        grid=(B, T // tl),
        in_specs=[act, gain, _mod_spec(sc, tl), _mod_spec(sh, tl)],
        out_specs=act,
        out_shape=jax.ShapeDtypeStruct((B, T, D), BF16),
        compiler_params=_cparams(2),
    )(x, g.reshape(1, D), sc, sh)


def _resnorm(x, y, gpost, gt, gpre=None, sc=None, sh=None):
    B, T, D = x.shape
    tl = _pick(T, 256, SUBLANES)
    act, gain = _row_specs(B, T, D, tl, 3)
    if gpre is None:
        return pl.pallas_call(
            _res_kernel,
            grid=(B, T // tl),
            in_specs=[act, act, gain, _mod_spec(gt, tl)],
            out_specs=act,
            out_shape=jax.ShapeDtypeStruct((B, T, D), F32),
            compiler_params=_cparams(2),
        )(x, y, gpost.reshape(1, D), gt), None
    return pl.pallas_call(
        _resnorm_kernel,
        grid=(B, T // tl),
        in_specs=[act, act, gain, _mod_spec(gt, tl), gain, _mod_spec(sc, tl), _mod_spec(sh, tl)],
        out_specs=[act, act],
        out_shape=[jax.ShapeDtypeStruct((B, T, D), F32), jax.ShapeDtypeStruct((B, T, D), BF16)],
        compiler_params=_cparams(2),
    )(x, y, gpost.reshape(1, D), gt, gpre.reshape(1, D), sc, sh)


def _merge_kernel(oa_ref, ob_ref, ga_ref, gb_ref, m_ref):
    m = jax.nn.sigmoid(ga_ref[0]) * oa_ref[0] + jax.nn.sigmoid(gb_ref[0]) * ob_ref[0]
    m_ref[0] = m.astype(m_ref.dtype)


def _merge(o_a, o_b, zg):
    B, T, W = o_a.shape
    tl = _pick(T, 256, SUBLANES)
    tc = _pick(W, 1024, LANES)
    nc = W // tc
    spec = pl.BlockSpec((1, tl, tc), lambda b, t, c: (b, t, c))
    return pl.pallas_call(
        _merge_kernel,
        grid=(B, T // tl, nc),
        in_specs=[spec, spec, spec, pl.BlockSpec((1, tl, tc), lambda b, t, c: (b, t, nc + c))],
        out_specs=spec,
        out_shape=jax.ShapeDtypeStruct((B, T, W), BF16),
        compiler_params=_cparams(3),
    )(o_a, o_b, zg, zg)


def _gates_kernel(z_ref, prm_ref, o_ref, carry_ref, *, H, cumulative, ot_ref=None):
    z = z_ref[0]
    tl = z.shape[0]
    a_log, dt_bias, b_f = prm_ref[0:1, :], prm_ref[1:2, :], prm_ref[2:3, :]
    lane = lax.broadcasted_iota(jnp.int32, z.shape, 1)
    beta = jax.nn.sigmoid(z)
    g = -jnp.exp(a_log) * _softplus(z + dt_bias)
    lf = -_softplus(-(z + b_f))
    if cumulative:
        row = lax.broadcasted_iota(jnp.int32, (tl, tl), 0)
        col = lax.broadcasted_iota(jnp.int32, (tl, tl), 1)
        tri = row >= col
        in_chunk = tri & ((row // CHUNK) == (col // CHUNK))
        g = jnp.dot(in_chunk.astype(F32), g, precision=HI, preferred_element_type=F32)

        @pl.when(pl.program_id(1) == 0)
        def _():
            carry_ref[...] = jnp.zeros_like(carry_ref)

        c = jnp.dot(tri.astype(F32), lf, precision=HI, preferred_element_type=F32) + carry_ref[...]
        carry_ref[...] = c[tl - 1:tl, :]
    else:
        c = lf
    c = pltpu.roll(c, H, axis=1)
    out = jnp.where(lane < H, beta, jnp.where(lane < 2 * H, g, jnp.where(lane < 3 * H, lf, c)))
    o_ref[0] = out
    if ot_ref is not None:
        ot_ref[0] = out.T


def _gates_kernel_2(z_ref, prm_ref, o_ref, ot_ref, carry_ref, *, H):
    _gates_kernel(z_ref, prm_ref, o_ref, carry_ref, H=H, cumulative=True, ot_ref=ot_ref)


def _gates(zs, prm, H, cumulative):
    B, T, _ = zs.shape
    tl = _pick(T, 256, LANES) if cumulative else T
    spec = pl.BlockSpec((1, tl, LANES), lambda b, t: (b, t, 0))
    out_specs, out_shape = spec, jax.ShapeDtypeStruct((B, T, LANES), F32)
    body = functools.partial(_gates_kernel, H=H, cumulative=False)
    if cumulative:
        assert tl % CHUNK == 0
        out_specs = [spec, pl.BlockSpec((1, LANES, tl), lambda b, t: (b, 0, t))]
        out_shape = [out_shape, jax.ShapeDtypeStruct((B, LANES, T), F32)]
        body = functools.partial(_gates_kernel_2, H=H)
    return pl.pallas_call(
        body,
        grid=(B, T // tl),
        in_specs=[spec, pl.BlockSpec((SUBLANES, LANES), lambda b, t: (0, 0))],
        out_specs=out_specs,
        out_shape=out_shape,
        scratch_shapes=[pltpu.VMEM((1, LANES), F32)],
        compiler_params=_cparams(2),
        name="gates",
    )(zs, prm)


def _lane_col(x, idx):
    lane = lax.broadcasted_iota(jnp.int32, x.shape, 1)
    return jnp.sum(jnp.where(lane == idx, x, 0.0), axis=1, keepdims=True)


def _bf16_dot(a, b):
    return jnp.dot(a.astype(BF16), b.astype(BF16), preferred_element_type=F32)


def _split_bf16(x):
    hi = x.astype(BF16)
    return hi, (x - hi.astype(F32)).astype(BF16)


def _dot_3pass(a, b):
    a_hi, a_lo = _split_bf16(a)
    b_hi, b_lo = _split_bf16(b)
    dot = functools.partial(jnp.dot, preferred_element_type=F32)
    return dot(a_hi, b_hi) + (dot(a_hi, b_lo) + dot(a_lo, b_hi))


def _each(f, *lists):
    return [f(*args) for args in zip(*lists)]


def _unit_lower_inverse(a_list, dot):
    n = a_list[0].shape[0]
    row = lax.broadcasted_iota(jnp.int32, (n, n), 0)
    col = lax.broadcasted_iota(jnp.int32, (n, n), 1)
    eye = (row == col).astype(F32)
    base = SUBLANES
    diag = (row // base) == (col // base)
    ad = _each(lambda a: jnp.where(diag, a, 0.0), a_list)
    a2 = _each(lambda x: dot(x, x), ad)
    a4 = _each(lambda x: dot(x, x), a2)
    t = _each(lambda x, y: dot(eye - x, eye + y), ad, a2)
    t = _each(lambda x, y: dot(x, eye + y), t, a4)
    s = base
    while s < n:
        off = ((row // (2 * s)) == (col // (2 * s))) & ((row // s) != (col // s))
        tm = _each(lambda x, a: dot(x, jnp.where(off, a, 0.0)), t, a_list)
        t = _each(lambda x, y: x - dot(y, x), t, tm)
        s *= 2
    return t


def _unit_lower_solve(a_list, rhs_list):
    t0 = _each(lambda t: t.astype(BF16), _unit_lower_inverse(a_list, _bf16_dot))
    x0 = _each(lambda t, b: jnp.dot(t, b.astype(BF16), preferred_element_type=F32), t0, rhs_list)
    r = _each(lambda a, b, x: b - x - _dot_3pass(a, x), a_list, rhs_list, x0)
    return _each(lambda t, x, rr: x + jnp.dot(t, rr.astype(BF16), preferred_element_type=F32), t0, x0, r)


def _conv_rows(xb, i, w_ref, n):
    ext = xb[i, 0:SUBLANES + n, :]
    acc = ext[SUBLANES:] * w_ref[0, CONV_W - 1:CONV_W, :]
    for s in range(1, CONV_W):
        back = pltpu.roll(ext, s, axis=0)[SUBLANES:]
        acc = acc + back * w_ref[0, CONV_W - 1 - s:CONV_W - s, :]
    return acc


def _l2n(t):
    return t * lax.rsqrt(jnp.sum(t * t, axis=-1, keepdims=True) + L2_EPS)


def _delta_prompt_kernel(zq_ref, zk_ref, zv_ref, wq_ref, wk_ref, wv_ref, gp_ref, gpt_ref, on_ref,
                         o_ref, sfin_ref, s_ref, xb, *, H, HB):
    hg = pl.program_id(1)
    t = pl.program_id(2)
    Lb = zq_ref.shape[1]

    @pl.when(t == 0)
    def _():
        s_ref[...] = jnp.zeros_like(s_ref)
        xb[:, 0:SUBLANES, :] = jnp.zeros((3, SUBLANES, HB * HD), F32)

    @pl.when(t > 0)
    def _():
        xb[:, 0:SUBLANES, :] = xb[:, Lb:Lb + SUBLANES, :]

    xb[0, SUBLANES:SUBLANES + Lb, :] = zq_ref[0]
    xb[1, SUBLANES:SUBLANES + Lb, :] = zk_ref[0]
    xb[2, SUBLANES:SUBLANES + Lb, :] = zv_ref[0]
    q_all = _silu(_conv_rows(xb, 0, wq_ref, Lb))
    k_all = _silu(_conv_rows(xb, 1, wk_ref, Lb))
    v_all = _silu(_conv_rows(xb, 2, wv_ref, Lb))

    gp = gp_ref[0]
    row = lax.broadcasted_iota(jnp.int32, (CHUNK, CHUNK), 0)
    col = lax.broadcasted_iota(jnp.int32, (CHUNK, CHUNK), 1)
    nc = Lb // CHUNK
    decay, kq_lhs, k_rhs, rhs_l, qg_l, kd_l, gl_l = [], [], [], [], [], [], []
    for j in range(HB):
        head = hg * HB + j
        hs = slice(j * HD, (j + 1) * HD)
        q = _l2n(q_all[:, hs]) * (HD ** -0.5)
        k = _l2n(k_all[:, hs])
        v = v_all[:, hs]
        beta = _lane_col(gp, head)
        gcum = _lane_col(gp, head + H)
        grow = gpt_ref[0, pl.ds(head + H, 1), :]
        eg = jnp.exp(gcum)
        kb = k * beta
        rhs = jnp.concatenate([v * beta, kb * eg], axis=1)
        qg = q * eg
        for c in range(nc):
            sl = slice(c * CHUNK, (c + 1) * CHUNK)
            gc = gcum[sl]
            g_last = gc[CHUNK - 1:CHUNK, :]
            decay.append(jnp.exp(jnp.where(row >= col, gc - grow[:, sl], -jnp.inf)))
            kq_lhs.append(jnp.concatenate([kb[sl], q[sl]], axis=0).astype(BF16))
            k_rhs.append(k[sl].astype(BF16))
            rhs_l.append(rhs[sl])
            qg_l.append(qg[sl])
            kd_l.append((k[sl] * jnp.exp(g_last - gc)).astype(BF16))
            gl_l.append(jnp.exp(g_last))
    kq = _each(_dot_nt, kq_lhs, k_rhs)
    a_mat = _each(lambda x, d: jnp.where(row > col, x[:CHUNK] * d, 0.0), kq, decay)
    a_qk = _each(lambda x, d: (x[CHUNK:] * d).astype(BF16), kq, decay)
    sol = _unit_lower_solve(a_mat, rhs_l)
    wq = _each(lambda x, g: jnp.concatenate([x[:, HD:], g], axis=0).astype(BF16), sol, qg_l)

    s = [s_ref[j] for j in range(HB)]
    outs = [[] for _ in range(HB)]
    for c in range(nc):
        idx = [j * nc + c for j in range(HB)]
        ws_qs = [jnp.dot(wq[i], s[j].astype(BF16), preferred_element_type=F32) for j, i in enumerate(idx)]
        db = [(sol[i][:, :HD] - ws_qs[j][:CHUNK]).astype(BF16) for j, i in enumerate(idx)]
        for j, i in enumerate(idx):
            outs[j].append(ws_qs[j][CHUNK:] + jnp.dot(a_qk[i], db[j], preferred_element_type=F32))
        s = [gl_l[i] * s[j] + _dot_tn(kd_l[i], db[j]) for j, i in enumerate(idx)]
    for j in range(HB):
        s_ref[j] = s[j]
        o_ref[0, :, j * HD:(j + 1) * HD] = _rms(jnp.concatenate(outs[j], axis=0), on_ref[...])

    @pl.when(t == pl.num_programs(2) - 1)
    def _():
        sfin_ref[0] = s_ref[...]


def _delta_prompt(z, gp, gpt, conv_w, onorm, layer, H, HB=8):
    B, L, _ = z.shape
    Lb = _pick(L, 256, CHUNK)
    HB = HB if H % HB == 0 else 1
    ng = H // HB

    def zspec(off):
        return pl.BlockSpec((1, Lb, HB * HD), lambda b, h, t: (b, t, off + h))

    def wspec(off):
        return pl.BlockSpec((1, CONV_W, HB * HD), lambda b, h, t: (layer, 0, off + h))

    return pl.pallas_call(
        functools.partial(_delta_prompt_kernel, H=H, HB=HB),
        grid=(B, ng, L // Lb),
        in_specs=[zspec(0), zspec(ng), zspec(2 * ng), wspec(0), wspec(ng), wspec(2 * ng),
                  pl.BlockSpec((1, Lb, LANES), lambda b, h, t: (b, t, 0)),
                  pl.BlockSpec((1, LANES, Lb), lambda b, h, t: (b, 0, t)),
                  pl.BlockSpec((1, HD), lambda b, h, t: (0, 0))],
        out_specs=[pl.BlockSpec((1, Lb, HB * HD), lambda b, h, t: (b, t, h)),
                   pl.BlockSpec((1, HB, HD, HD), lambda b, h, t: (b, h, 0, 0))],
        out_shape=[jax.ShapeDtypeStruct((B, L, H * HD), F32), jax.ShapeDtypeStruct((B, H, HD, HD), F32)],
        scratch_shapes=[pltpu.VMEM((HB, HD, HD), F32), pltpu.VMEM((3, Lb + SUBLANES, HB * HD), F32)],
        compiler_params=_cparams(3),
        name="delta_prompt",
    )(z, z, z, conv_w, conv_w, conv_w, gp, gpt, onorm)


def _delta_sample_kernel(zq_ref, zk_ref, zv_ref, bq_ref, bk_ref, bv_ref, wq_ref, wk_ref, wv_ref, gp_ref, on_ref,
                         s0_ref, o_ref, s_ref, *, H, HB):
    hg = pl.program_id(1)

    def conv(z_ref, b_ref, w_ref):
        acc = z_ref[0] * w_ref[0, CONV_W - 1:CONV_W, :]
        for j in range(CONV_W - 1):
            acc = acc + b_ref[0, 0, j:j + 1, :] * w_ref[0, j:j + 1, :]
        return _silu(acc)

    q_all = conv(zq_ref, bq_ref, wq_ref)
    k_all = conv(zk_ref, bk_ref, wk_ref)
    v_all = conv(zv_ref, bv_ref, wv_ref)
    gp = gp_ref[0]
    hs = [slice(j * HD, (j + 1) * HD) for j in range(HB)]
    q = [_l2n(q_all[:, sl]) * (HD ** -0.5) for sl in hs]
    k = [_l2n(k_all[:, sl]) for sl in hs]
    v = [v_all[:, sl] for sl in hs]
    beta = [_lane_col(gp, hg * HB + j) for j in range(HB)]
    g = [_lane_col(gp, hg * HB + j + H) for j in range(HB)]
    dot = functools.partial(jnp.dot, precision=HI, preferred_element_type=F32)
    row = lax.broadcasted_iota(jnp.int32, (HD, HD), 0)
    s = [jnp.exp(g[j]) * s0_ref[0, 0, j] for j in range(HB)]
    ks = _each(lambda kk, ss: dot(jnp.broadcast_to(kk, (SUBLANES, HD)), ss)[0:1], k, s)
    delta = _each(lambda b, vv, x: b * (vv - x), beta, v, ks)
    outer = _each(lambda kk, dd: _dot_tn(jnp.where(row == 0, jnp.broadcast_to(kk, (HD, HD)), 0.0),
                                         jnp.broadcast_to(dd, (HD, HD)), precision=HI), k, delta)
    s = _each(lambda a, b: a + b, s, outer)
    o = _each(lambda qq, ss: dot(jnp.broadcast_to(qq, (SUBLANES, HD)), ss)[0:1], q, s)
    for j in range(HB):
        s_ref[0, j] = s[j]
        o_ref[0, :, hs[j]] = _rms(o[j], on_ref[...])


def _delta_sample(z3, state_conv, state_delta, gp, conv_w, onorm, layer, H, HB=8):
    B = z3.shape[0]
    HB = HB if H % HB == 0 else 1
    ng = H // HB

    def zspec(off):
        return pl.BlockSpec((1, 1, HB * HD), lambda b, h: (b, 0, off + h))

    def bspec(off):
        return pl.BlockSpec((1, 1, CONV_W - 1, HB * HD), lambda b, h: (b, layer, 0, off + h))

    def wspec(off):
        return pl.BlockSpec((1, CONV_W, HB * HD), lambda b, h: (layer, 0, off + h))

    return pl.pallas_call(
        functools.partial(_delta_sample_kernel, H=H, HB=HB),
        grid=(B, ng),
        in_specs=[zspec(0), zspec(ng), zspec(2 * ng), bspec(0), bspec(ng), bspec(2 * ng),
                  wspec(0), wspec(ng), wspec(2 * ng),
                  pl.BlockSpec((1, 1, LANES), lambda b, h: (b, 0, 0)),
                  pl.BlockSpec((1, HD), lambda b, h: (0, 0)),
                  pl.BlockSpec((1, 1, HB, HD, HD), lambda b, h: (b, layer, h, 0, 0))],
        out_specs=[pl.BlockSpec((1, 1, HB * HD), lambda b, h: (b, 0, h)),
                   pl.BlockSpec((1, HB, HD, HD), lambda b, h: (b, h, 0, 0))],
        out_shape=[jax.ShapeDtypeStruct((B, 1, H * HD), F32), jax.ShapeDtypeStruct((B, H, HD, HD), F32)],
        compiler_params=_cparams(2),
        name="delta_sample",
    )(z3, z3, z3, state_conv, state_conv, state_conv, conv_w, conv_w, conv_w, gp, onorm, state_delta)


def _fox_prompt_kernel(q_ref, k_ref, v_ref, gpt_ref, oa_ref, ga_ref, gb_ref, o_ref, kb_ref, va_ref, *, H, G):
    kvh = pl.program_id(1)
    qi = pl.program_id(2)
    tq = q_ref.shape[1]
    L = k_ref.shape[1]

    @pl.when(qi == 0)
    def _():
        kb_ref[...] = k_ref[0].astype(BF16)
        va_ref[:, :HD] = v_ref[0].astype(BF16)
        va_ref[:, HD:] = jnp.ones((L, HD), BF16)

    causal = lax.broadcasted_iota(jnp.int32, (tq, tq), 1) <= lax.broadcasted_iota(jnp.int32, (tq, tq), 0)
    scale = HD ** -0.5 * LOG2E
    dot = functools.partial(jnp.dot, preferred_element_type=F32)

    def attend(lo):
        for g in range(G):
            sl = slice(g * HD, (g + 1) * HD)
            qb = q_ref[0, :, sl].astype(BF16)
            ck = gpt_ref[0, pl.ds(3 * H + kvh * G + g, 1), :] * LOG2E
            s_own = jnp.where(causal, _dot_nt(qb, kb_ref[lo:lo + tq, :]) * scale - ck[:, lo:lo + tq], -jnp.inf)
            m = jnp.max(s_own, axis=1, keepdims=True)
            if lo:
                s_past = _dot_nt(qb, kb_ref[0:lo, :]) * scale - ck[:, 0:lo]
                m = jnp.maximum(m, jnp.max(s_past, axis=1, keepdims=True))
            pv = dot(jnp.exp2(s_own - m).astype(BF16), va_ref[lo:lo + tq, :])
            if lo:
                pv = pv + dot(jnp.exp2(s_past - m).astype(BF16), va_ref[0:lo, :])
            o_b = pv[:, :HD] / pv[:, HD:]
            o_ref[0, :, sl] = (jax.nn.sigmoid(ga_ref[0, :, sl]) * oa_ref[0, :, sl]
                               + jax.nn.sigmoid(gb_ref[0, :, sl]) * o_b).astype(o_ref.dtype)

    for n in range(L // tq):
        pl.when(qi == n)(functools.partial(attend, n * tq))


def _fox_prompt(z, gpt, o_a, zg, H, KV, q_col, k_col, v_col):
    B, L, _ = z.shape
    D = H * HD
    G = H // KV
    tq = _pick(L, 512, LANES)
    qw = G * HD
    blk = pl.BlockSpec((1, tq, qw), lambda b, kv, qi: (b, qi, kv))
    return pl.pallas_call(
        functools.partial(_fox_prompt_kernel, H=H, G=G),
        grid=(B, KV, L // tq),
        in_specs=[pl.BlockSpec((1, tq, qw), lambda b, kv, qi: (b, qi, q_col // qw + kv)),
                  pl.BlockSpec((1, L, HD), lambda b, kv, qi: (b, 0, k_col // HD + kv)),
                  pl.BlockSpec((1, L, HD), lambda b, kv, qi: (b, 0, v_col // HD + kv)),
                  pl.BlockSpec((1, LANES, L), lambda b, kv, qi: (b, 0, 0)),
                  blk, blk,
                  pl.BlockSpec((1, tq, qw), lambda b, kv, qi: (b, qi, D // qw + kv))],
        out_specs=blk,
        out_shape=jax.ShapeDtypeStruct((B, L, D), BF16),
        scratch_shapes=[pltpu.VMEM((L, HD), BF16), pltpu.VMEM((L, 2 * HD), BF16)],
        compiler_params=_cparams(3),
        name="fox_prompt",
    )(z, z, z, gpt, o_a, zg, zg)


def _fox_sample_kernel(pt_ref, q_ref, kn_ref, vn_ref, gp_ref, *refs, H, G, PP):
    k_refs, v_refs, lf_refs = refs[:PP], refs[PP:2 * PP], refs[2 * PP:3 * PP]
    o_ref, m_ref, l_ref, acc_ref, carry_ref, later_ref = refs[3 * PP:]
    p = pl.program_id(1)
    KV = H // G
    P = k_refs[0].shape[0]
    W = P * KV

    @pl.when(p == 0)
    def _():
        m_ref[...] = jnp.full(m_ref.shape, -jnp.inf, F32)
        l_ref[...] = jnp.zeros_like(l_ref)
        acc_ref[...] = jnp.zeros_like(acc_ref)
        r2 = lax.broadcasted_iota(jnp.int32, (H, LANES), 0)
        c2 = lax.broadcasted_iota(jnp.int32, (H, LANES), 1)
        lf_new = jnp.broadcast_to(gp_ref[0], (H, LANES))
        carry_ref[...] = jnp.sum(jnp.where(c2 == r2 + 2 * H, lf_new, 0.0), axis=1, keepdims=True)
        later_ref[...] = (lax.broadcasted_iota(jnp.int32, (P, W), 0)
                          > lax.broadcasted_iota(jnp.int32, (P, W), 1) // KV).astype(BF16)

    own = (lax.rem(lax.broadcasted_iota(jnp.int32, (H, W), 1), KV)
           == lax.broadcasted_iota(jnp.int32, (H, W), 0) // G)
    qb = q_ref[0].astype(BF16)
    later = later_ref[...]
    dot = functools.partial(jnp.dot, preferred_element_type=F32)
    lf = [lf_refs[i][...] for i in range(PP)]
    lf_hi = _each(lambda x: x.astype(BF16), lf)
    lf_r1 = _each(lambda x, h: x - h.astype(F32), lf, lf_hi)
    lf_mid = _each(lambda x: x.astype(BF16), lf_r1)
    lf_lo = _each(lambda x, m: (x - m.astype(F32)).astype(BF16), lf_r1, lf_mid)
    bias = _each(lambda a, b, c: dot(a, later) + (dot(b, later) + dot(c, later)), lf_hi, lf_mid, lf_lo)
    qk = [_dot_nt(qb, k_refs[i][...].reshape(W, HD).astype(BF16)) for i in range(PP)]
    carry = carry_ref[...]
    s = []
    for i in range(PP):
        s.append(jnp.where(own, qk[i] * (HD ** -0.5) + (bias[i] + carry), -jnp.inf))
        carry = carry + jnp.sum(lf[i], axis=1, keepdims=True)
    carry_ref[...] = carry
    m_old = m_ref[...]
    m_new = m_old
    for x in s:
        m_new = jnp.maximum(m_new, jnp.max(x, axis=1, keepdims=True))
    alpha = jnp.exp(m_old - m_new)
    pr = _each(lambda x: jnp.exp(x - m_new), s)
    l_new = alpha * l_ref[...]
    for x in pr:
        l_new = l_new + jnp.sum(x, axis=1, keepdims=True)
    l_ref[...] = l_new
    pv = [dot(pr[i].astype(BF16), v_refs[i][...].reshape(W, HD).astype(BF16)) for i in range(PP)]
    acc = alpha * acc_ref[...]
    for x in pv:
        acc = acc + x
    acc_ref[...] = acc
    m_ref[...] = m_new

    @pl.when(p == pl.num_programs(1) - 1)
    def _():
        s_new = jnp.sum(q_ref[0] * kn_ref[0], axis=1, keepdims=True) * (HD ** -0.5)
        m_old = m_ref[...]
        m_new = jnp.maximum(m_old, s_new)
        alpha = jnp.exp(m_old - m_new)
        p_new = jnp.exp(s_new - m_new)
        o_ref[0] = (alpha * acc_ref[...] + p_new * vn_ref[0]) / (alpha * l_ref[...] + p_new)


def _fox_sample(q, k_new, v_new, gp, cache_k, cache_v, cache_logf_t, page_table, layer, H, KV, PP=8):
    B, n_pages = page_table.shape
    P = cache_k.shape[2]
    G = H // KV
    while n_pages % PP:
        PP -= 1

    def page_spec(shape, i):
        return pl.BlockSpec((None, None) + shape, lambda b, p, pt: (pt[b, n_pages - 1 - (p * PP + i)], layer, 0, 0, 0)[:2 + len(shape)])

    row = pl.BlockSpec((1, H, HD), lambda b, p, pt: (b, 0, 0))
    grid_spec = pltpu.PrefetchScalarGridSpec(
        num_scalar_prefetch=1,
        grid=(B, n_pages // PP),
        in_specs=([row, row, row, pl.BlockSpec((1, 1, LANES), lambda b, p, pt: (b, 0, 0))]
                  + [page_spec((P, KV, HD), i) for i in range(PP)]
                  + [page_spec((P, KV, HD), i) for i in range(PP)]
                  + [page_spec((H, P), i) for i in range(PP)]),
        out_specs=row,
        scratch_shapes=[pltpu.VMEM((H, 1), F32), pltpu.VMEM((H, 1), F32), pltpu.VMEM((H, HD), F32),
                        pltpu.VMEM((H, 1), F32), pltpu.VMEM((P, P * KV), BF16)],
    )
    return pl.pallas_call(
        functools.partial(_fox_sample_kernel, H=H, G=G, PP=PP),
        grid_spec=grid_spec,
        out_shape=jax.ShapeDtypeStruct((B, H, HD), F32),
        compiler_params=_cparams(2),
        name="fox_sample",
    )(page_table, q, k_new, v_new, gp, *([cache_k] * PP), *([cache_v] * PP), *([cache_logf_t] * PP))


def _split6(m):
    return tuple(jnp.split(m, 6, axis=-1))


def kernel(x_prompt, x_sample, cache_k, cache_v, cache_logf, state_delta, state_conv, page_table, c_prompt, c_sample,
           norm_g, w_ada, b_ada, w_in, conv_w, a_log, dt_bias, b_f, onorm_g, w_o, w_gu, w_down):
    B, L, D = x_prompt.shape
    BS = x_sample.shape[0]
    depth = w_in.shape[0]
    H = D // HD
    KV = cache_k.shape[3]
    KVW = KV * HD
    F = w_down.shape[1]
    c_main = 3 * D + D + 2 * KVW
    c_gate = c_main + 3 * H
    q_col, k_col, v_col = 3 * D, 4 * D, 4 * D + KVW
    assert 4 * H <= LANES and w_in.shape[2] == c_gate + 2 * D

    n_c = B + BS
    pad = (-n_c) % SUBLANES
    c_all = jnp.concatenate([c_prompt, c_sample, jnp.zeros((pad, D), F32)], axis=0)
    mod = _adaln(c_all, w_ada, b_ada)

    w_in_t = jnp.swapaxes(w_in, 1, 2)
    zl = jnp.zeros((depth, H), F32)
    prm = jnp.stack([jnp.concatenate([zl, a_log, zl], 1), jnp.concatenate([zl, dt_bias, zl], 1),
                     jnp.concatenate([zl, zl, b_f], 1)], axis=1)
    prm = jnp.pad(prm, ((0, 0), (0, SUBLANES - 3), (0, LANES - 3 * H)))
    cache_logf_t = jnp.swapaxes(cache_logf, 2, 3)
    w_down_b = w_down.astype(BF16)

    xp, xs = x_prompt, x_sample.reshape(1, BS, D)
    st_p = ([], [], [], [], [])
    st_s = ([], [], [], [], [])
    hp = hs = None
    for l in range(depth):
        mp = [m.reshape(B, 1, D) for m in _split6(mod[l, :B])]
        ms = [m.reshape(1, BS, D) for m in _split6(mod[l, B:n_c])]
        if l == 0:
            hp = _prenorm(xp, norm_g[l, 0], mp[1], mp[0])
            hs = _prenorm(xs, norm_g[l, 0], ms[1], ms[0])
        onorm = onorm_g[l].reshape(1, HD)

        win = functools.partial(_linear, hp.reshape(B * L, D), hs.reshape(BS, D), w_in_t, l, nt=True)
        z, z_s = win(start=0, size=c_main)
        zs, zs_s = win(start=c_main, size=LANES)
        zg, zg_s = win(start=c_gate, size=2 * D)

        z = z.reshape(B, L, c_main)
        gp, gpt = _gates(zs.reshape(B, L, LANES), prm[l], H, True)
        o_a, s_fin = _delta_prompt(z, gp, gpt, conv_w, onorm, l, H)
        m = _fox_prompt(z, gpt, o_a, zg.reshape(B, L, 2 * D), H, KV, q_col, k_col, v_col)
        st_p[0].append(z[:, :, k_col:k_col + KVW].reshape(B, L, KV, HD))
        st_p[1].append(z[:, :, v_col:v_col + KVW].reshape(B, L, KV, HD))
        st_p[2].append(gp[:, :, 2 * H:3 * H])
        st_p[3].append(s_fin)
        st_p[4].append(z[:, L - (CONV_W - 1):, :3 * D])

        gp = _gates(zs_s.reshape(BS, 1, LANES), prm[l], H, False)
        z3 = z_s.reshape(BS, 1, c_main)
        o_a, s_new = _delta_sample(z3, state_conv, state_delta, gp, conv_w, onorm, l, H)
        k_rep = jnp.repeat(z_s[:, k_col:k_col + KVW].reshape(BS, KV, HD), H // KV, axis=1)
        v_rep = jnp.repeat(z_s[:, v_col:v_col + KVW].reshape(BS, KV, HD), H // KV, axis=1)
        o_b = _fox_sample(z_s[:, q_col:q_col + D].reshape(BS, H, HD), k_rep, v_rep, gp, cache_k, cache_v,
                          cache_logf_t, page_table, l, H, KV)
        m_s = _merge(o_a.reshape(1, BS, D), o_b.reshape(1, BS, D), zg_s.reshape(1, BS, 2 * D))
        st_s[0].append(z_s[:, k_col:k_col + KVW].reshape(BS, 1, KV, HD))
        st_s[1].append(z_s[:, v_col:v_col + KVW].reshape(BS, 1, KV, HD))
        st_s[2].append(gp[:, :, 2 * H:3 * H])
        st_s[3].append(s_new)
        st_s[4].append(jnp.concatenate([state_conv[:, l, 1:], z3[:, :, :3 * D]], axis=1))

        y, y_s = _linear(m.reshape(B * L, D), m_s.reshape(BS, D), w_o, l)
        xp, hf = _resnorm(xp, y.reshape(B, L, D), norm_g[l, 1], mp[2], norm_g[l, 2], mp[4], mp[3])
        xs, hf_s = _resnorm(xs, y_s.reshape(1, BS, D), norm_g[l, 1], ms[2], norm_g[l, 2], ms[4], ms[3])
        act, act_s = _linear(hf.reshape(B * L, D), hf_s.reshape(BS, D), w_gu, l, swiglu=True, out_dtype=BF16, tn=256)
        y, y_s = _linear(act, act_s, w_down_b, l, tm=512)
        y, y_s = y.reshape(B, L, D), y_s.reshape(1, BS, D)
        if l + 1 < depth:
            mnext = [mm.reshape(B, 1, D) for mm in _split6(mod[l + 1, :B])]
            xp, hp = _resnorm(xp, y, norm_g[l, 3], mp[5], norm_g[l + 1, 0], mnext[1], mnext[0])
            mnext = [mm.reshape(1, BS, D) for mm in _split6(mod[l + 1, B:n_c])]
            xs, hs = _resnorm(xs, y_s, norm_g[l, 3], ms[5], norm_g[l + 1, 0], mnext[1], mnext[0])
        else:
            xp, _ = _resnorm(xp, y, norm_g[l, 3], mp[5])
            xs, _ = _resnorm(xs, y_s, norm_g[l, 3], ms[5])

    k_p, v_p, lf_p, d_p, cv_p = (jnp.stack(lst, axis=1) for lst in st_p)
    k_s, v_s, lf_s, d_s, cv_s = (jnp.stack(lst, axis=1) for lst in st_s)
    return (xp, xs.reshape(BS, 1, D), k_p, v_p, lf_p, d_p, cv_p, k_s, v_s, lf_s, d_s, cv_s)
```
